```python
import jax, jax.numpy as jnp
from jax import lax
import numpy as np

D_MODEL = 1024
BATCH = 8
SEQ = 2048
DEPTH = 1

RET_HEADS = 8
RET_QK_DIM = 64
RET_V_DIM = 128
RET_QK_WIDTH = RET_HEADS * RET_QK_DIM
RET_WIDTH = RET_HEADS * RET_V_DIM
CHUNK = 128
ROPE_THETA = 10000.0
CONV_WIDTH = D_MODEL
CONV_KERNEL = 31
MIX_WIDTH = RET_WIDTH + CONV_WIDTH
IN_WIDTH = 2 * RET_QK_WIDTH + 2 * RET_WIDTH + 3 * CONV_WIDTH
LN_EPS = 1e-5
DEEPNORM_ALPHA = (2.0 * DEPTH) ** 0.25
DEEPNORM_BETA = (8.0 * DEPTH) ** -0.25

kernel_name = "hybrid_retention_conformer_parallel"


def _layernorm(x, g, b):
    xf = x.astype(jnp.float32)
    mu = jnp.mean(xf, axis=-1, keepdims=True)
    var = jnp.mean(jnp.square(xf - mu), axis=-1, keepdims=True)
    y = (xf - mu) * lax.rsqrt(var + LN_EPS)
    return (y * g.astype(jnp.float32) + b.astype(jnp.float32)).astype(x.dtype)


def _rotary(t, pos):
    d = t.shape[-1]
    inv_freq = ROPE_THETA ** (-jnp.arange(0, d // 2, dtype=jnp.float32) * 2.0 / d)
    ang = pos[:, None] * inv_freq[None, :]
    cos = jnp.cos(ang)[None, :, None, :].astype(t.dtype)
    sin = jnp.sin(ang)[None, :, None, :].astype(t.dtype)
    t1, t2 = t[..., : d // 2], t[..., d // 2:]
    return jnp.concatenate([t1 * cos - t2 * sin, t1 * sin + t2 * cos], axis=-1)


def _retention(q, k, v):
    b, s, h, dk = q.shape
    dv = v.shape[-1]
    nc = s // CHUNK
    log_g = jnp.log1p(-jnp.exp2(-5.0 - jnp.arange(h, dtype=jnp.float32)))
    idx = jnp.arange(CHUNK, dtype=jnp.float32)
    rel = idx[:, None] - idx[None, :]
    intra_decay = jnp.where(rel[None] >= 0,
                            jnp.exp(log_g[:, None, None] * jnp.maximum(rel, 0.0)[None]), 0.0)
    q_decay = jnp.exp(log_g[:, None] * (idx[None, :] + 1.0))
    k_decay = jnp.exp(log_g[:, None] * (CHUNK - 1.0 - idx[None, :]))
    chunk_decay = jnp.exp(log_g * CHUNK)

    qc = q.astype(jnp.float32).reshape(b, nc, CHUNK, h, dk)
    kc = k.astype(jnp.float32).reshape(b, nc, CHUNK, h, dk) * (dk ** -0.5)
    vc = v.astype(jnp.float32).reshape(b, nc, CHUNK, h, dv)

    scores = jnp.einsum('bnihd,bnjhd->bnhij', qc, kc) * intra_decay[None, None]
    inner = jnp.einsum('bnhij,bnjhe->bnihe', scores, vc)

    kv = jnp.einsum('bnjhd,hj,bnjhe->bnhde', kc, k_decay, vc)

    def step(state, kv_n):
        return chunk_decay[None, :, None, None] * state + kv_n, state

    _, prev = lax.scan(step, jnp.zeros((b, h, dk, dv), jnp.float32), jnp.moveaxis(kv, 1, 0))
    prev = jnp.moveaxis(prev, 0, 1)
    cross = jnp.einsum('bnihd,hi,bnhde->bnihe', qc, q_decay, prev)
    return (inner + cross).reshape(b, s, h, dv).astype(q.dtype)


def _head_groupnorm(y, g):
    yf = y.astype(jnp.float32)
    mu = jnp.mean(yf, axis=-1, keepdims=True)
    var = jnp.mean(jnp.square(yf - mu), axis=-1, keepdims=True)
    yn = (yf - mu) * lax.rsqrt(var + LN_EPS)
    b, s, h, dv = y.shape
    return (yn.reshape(b, s, h * dv) * g.astype(jnp.float32)).astype(y.dtype)


def _layer(x, w_in, ret_norm_g, dw_kernel, dw_bias, conv_ln_g, conv_ln_b,
           w_pw2, b_pw2, w_out, post_ln_g, post_ln_b):
    b, s, _ = x.shape
    z = jnp.einsum('bsd,df->bsf', x, w_in)
    o = np.cumsum([0, RET_QK_WIDTH, RET_QK_WIDTH, RET_WIDTH, RET_WIDTH,
                   CONV_WIDTH, CONV_WIDTH, CONV_WIDTH])
    q, k, v, g_ret, a_conv, glu_conv, g_conv = [z[..., o[i]:o[i + 1]] for i in range(7)]

    pos = jnp.arange(s, dtype=jnp.float32)
    q = _rotary(q.reshape(b, s, RET_HEADS, RET_QK_DIM), pos)
    k = _rotary(k.reshape(b, s, RET_HEADS, RET_QK_DIM), pos)
    v = v.reshape(b, s, RET_HEADS, RET_V_DIM)
    ret = _head_groupnorm(_retention(q, k, v), ret_norm_g) * jax.nn.silu(g_ret)

    u = a_conv * jax.nn.sigmoid(glu_conv)
    u = lax.conv_general_dilated(
        u, dw_kernel[:, None, :].astype(u.dtype), window_strides=(1,),
        padding=[(CONV_KERNEL - 1, 0)], dimension_numbers=('NWC', 'WIO', 'NWC'),
        feature_group_count=CONV_WIDTH) + dw_bias
    u = jax.nn.silu(_layernorm(u, conv_ln_g, conv_ln_b))
    u = jnp.einsum('bsc,ce->bse', u, w_pw2) + b_pw2
    conv = u * jax.nn.silu(g_conv)

    hmix = jnp.einsum('bsm,md->bsd', jnp.concatenate([ret, conv], axis=-1), w_out)
    return _layernorm(DEEPNORM_ALPHA * x + hmix, post_ln_g, post_ln_b)


def setup_inputs(seed: int = 0) -> dict:
    key = jax.random.key(seed)
    ks = jax.random.split(key, 12)
    f32 = jnp.float32
    x = jax.random.normal(ks[0], (BATCH, SEQ, D_MODEL), f32)
    col_scale = np.ones((IN_WIDTH,), np.float32)
    v0 = 2 * RET_QK_WIDTH
    col_scale[v0:v0 + RET_WIDTH] = DEEPNORM_BETA
    c0 = 2 * RET_QK_WIDTH + 2 * RET_WIDTH
    col_scale[c0:c0 + CONV_WIDTH] = DEEPNORM_BETA
    w_in = jax.random.normal(ks[1], (DEPTH, D_MODEL, IN_WIDTH), f32) * (D_MODEL ** -0.5) * jnp.asarray(col_scale)
    ret_norm_g = 1.0 + 0.01 * jax.random.normal(ks[2], (DEPTH, RET_WIDTH), f32)
    dw_kernel = jax.random.normal(ks[3], (DEPTH, CONV_KERNEL, CONV_WIDTH), f32) * (CONV_KERNEL ** -0.5)
    dw_bias = 0.01 * jax.random.normal(ks[4], (DEPTH, CONV_WIDTH), f32)
    conv_ln_g = 1.0 + 0.01 * jax.random.normal(ks[5], (DEPTH, CONV_WIDTH), f32)
    conv_ln_b = 0.01 * jax.random.normal(ks[6], (DEPTH, CONV_WIDTH), f32)
    w_pw2 = jax.random.normal(ks[7], (DEPTH, CONV_WIDTH, CONV_WIDTH), f32) * (CONV_WIDTH ** -0.5) * DEEPNORM_BETA
    b_pw2 = 0.01 * jax.random.normal(ks[8], (DEPTH, CONV_WIDTH), f32)
    w_out = jax.random.normal(ks[9], (DEPTH, MIX_WIDTH, D_MODEL), f32) * (MIX_WIDTH ** -0.5) * DEEPNORM_BETA
    post_ln_g = 1.0 + 0.01 * jax.random.normal(ks[10], (DEPTH, D_MODEL), f32)
    post_ln_b = 0.01 * jax.random.normal(ks[11], (DEPTH, D_MODEL), f32)
    return {"x": x, "w_in": w_in, "ret_norm_g": ret_norm_g, "dw_kernel": dw_kernel,
            "dw_bias": dw_bias, "conv_ln_g": conv_ln_g, "conv_ln_b": conv_ln_b,
            "w_pw2": w_pw2, "b_pw2": b_pw2, "w_out": w_out,
            "post_ln_g": post_ln_g, "post_ln_b": post_ln_b}


def reference(x, w_in, ret_norm_g, dw_kernel, dw_bias, conv_ln_g, conv_ln_b,
              w_pw2, b_pw2, w_out, post_ln_g, post_ln_b):
    for layer in range(DEPTH):
        x = _layer(x, w_in[layer], ret_norm_g[layer], dw_kernel[layer], dw_bias[layer],
                   conv_ln_g[layer], conv_ln_b[layer], w_pw2[layer], b_pw2[layer],
                   w_out[layer], post_ln_g[layer], post_ln_b[layer])
    return x
```

```python
import functools

import numpy as np
import jax
import jax.numpy as jnp
from jax import lax
from jax.experimental import pallas as pl
from jax.experimental.pallas import tpu as pltpu

D_MODEL = 1024
RET_HEADS = 8
RET_QK_DIM = 64
RET_V_DIM = 128
RET_QK_WIDTH = RET_HEADS * RET_QK_DIM
RET_WIDTH = RET_HEADS * RET_V_DIM
CHUNK = 128
ROPE_THETA = 10000.0
CONV_WIDTH = D_MODEL
CONV_KERNEL = 31
LN_EPS = 1e-5
DEPTH = 1
DEEPNORM_ALPHA = (2.0 * DEPTH) ** 0.25

LANES = 128
HEAD_PAIRS = RET_HEADS // 2
HALO = 32
SEQ_TILE = 256
VMEM_LIMIT_BYTES = 56 * 1024 * 1024

_OFF = np.cumsum([0, RET_QK_WIDTH, RET_QK_WIDTH, RET_WIDTH, RET_WIDTH, CONV_WIDTH, CONV_WIDTH, CONV_WIDTH])


def _decay_tables():
    log_g = np.log1p(-np.exp2(-5.0 - np.arange(RET_HEADS, dtype=np.float64)))
    lane_head = np.arange(LANES) // RET_QK_DIM
    idx = np.arange(CHUNK, dtype=np.float64)
    scale = RET_QK_DIM ** -0.5
    qdec = np.zeros((HEAD_PAIRS, CHUNK, LANES))
    kdec_in = np.zeros((HEAD_PAIRS, CHUNK, LANES))
    kdec_st = np.zeros((HEAD_PAIRS, CHUNK, LANES))
    cdec = np.zeros((HEAD_PAIRS, LANES, RET_V_DIM))
    for p in range(HEAD_PAIRS):
        lg = log_g[2 * p + lane_head]
        qdec[p] = np.exp(lg[None, :] * (idx[:, None] + 1.0))
        kdec_in[p] = scale * np.exp(-lg[None, :] * (idx[:, None] + 1.0))
        kdec_st[p] = scale * np.exp(lg[None, :] * (CHUNK - 1.0 - idx[:, None]))
        cdec[p] = np.exp(lg * CHUNK)[:, None] * np.ones((1, RET_V_DIM))
    f = lambda a: jnp.asarray(a, jnp.float32)
    return f(qdec), f(kdec_in), f(kdec_st), f(cdec)


def _rope_tables(seq):
    half = RET_QK_DIM // 2
    inv_freq = ROPE_THETA ** (-np.arange(0, half, dtype=np.float64) * 2.0 / RET_QK_DIM)
    ang = np.arange(seq, dtype=np.float64)[:, None] * inv_freq[None, :]
    lane = np.arange(LANES)
    cos = np.cos(ang)[:, lane % half]
    sign = np.where((lane % RET_QK_DIM) < half, -1.0, 1.0)
    sin = np.sin(ang)[:, lane % half] * sign[None, :]
    return jnp.asarray(cos, jnp.float32), jnp.asarray(sin, jnp.float32)


def _silu(x):
    return x * jax.nn.sigmoid(x)


def _layernorm_rows(x, g, b):
    mu = jnp.mean(x, axis=-1, keepdims=True)
    d = x - mu
    var = jnp.mean(d * d, axis=-1, keepdims=True)
    return d * lax.rsqrt(var + LN_EPS) * g + b


def _layer_kernel(x_ref, win_ref, wpw_ref, wout_ref, cos_ref, sin_ref,
                  qdec_ref, kdin_ref, kdst_ref, cdec_ref,
                  retg_ref, dwk_ref, dwb_ref, clng_ref, clnb_ref, bpw_ref, plng_ref, plnb_ref,
                  o_ref, state_ref, ubuf_ref, ret_ref):
    tile = x_ref.shape[0]
    f32, bf16 = jnp.float32, jnp.bfloat16

    @pl.when(pl.program_id(1) == 0)
    def _():
        state_ref[...] = jnp.zeros_like(state_ref)
        ubuf_ref[0:HALO, :] = jnp.zeros((HALO, CONV_WIDTH), f32)

    x = x_ref[...]
    xb = x.astype(bf16)

    def proj(i):
        return jnp.dot(xb, win_ref[:, _OFF[i]:_OFF[i + 1]], preferred_element_type=f32)

    cos = cos_ref[...]
    sin = sin_ref[...]
    lane = lax.broadcasted_iota(jnp.int32, (tile, LANES), 1)
    first_half = (lane % RET_QK_DIM) < (RET_QK_DIM // 2)

    def rotary(t):
        blocks = []
        for p in range(HEAD_PAIRS):
            blk = t[:, p * LANES:(p + 1) * LANES]
            swapped = jnp.where(first_half,
                                pltpu.roll(blk, LANES - RET_QK_DIM // 2, axis=1),
                                pltpu.roll(blk, RET_QK_DIM // 2, axis=1))
            blocks.append(blk * cos + swapped * sin)
        return blocks

    q_blocks = rotary(proj(0))
    k_blocks = rotary(proj(1))
    v = proj(2).astype(bf16)

    row = lax.broadcasted_iota(jnp.int32, (CHUNK, CHUNK), 0)
    col = lax.broadcasted_iota(jnp.int32, (CHUNK, CHUNK), 1)
    causal = row >= col
    head0_lane = col < RET_QK_DIM
    head0_row = row < RET_QK_DIM
    for c in range(tile // CHUNK):
        rows = slice(c * CHUNK, (c + 1) * CHUNK)
        for p in range(HEAD_PAIRS):
            qp = q_blocks[p][rows] * qdec_ref[p]
            kp = k_blocks[p][rows]
            k_in = (kp * kdin_ref[p]).astype(bf16)
            k_st = (kp * kdst_ref[p]).astype(bf16)
            vp = v[rows, 2 * p * RET_V_DIM:(2 * p + 2) * RET_V_DIM]
            state = state_ref[p]
            state_b = state.astype(bf16)
            for e in range(2):
                qm = jnp.where(head0_lane if e == 0 else ~head0_lane, qp, 0.0).astype(bf16)
                s = lax.dot_general(qm, k_in, (((1,), (1,)), ((), ())), preferred_element_type=f32)
                pm = jnp.where(causal, s, 0.0).astype(bf16)
                o = (jnp.dot(pm, vp[:, e * RET_V_DIM:(e + 1) * RET_V_DIM], preferred_element_type=f32)
                     + jnp.dot(qm, state_b, preferred_element_type=f32))
                h = 2 * p + e
                ret_ref[rows, h * RET_V_DIM:(h + 1) * RET_V_DIM] = o
            kv = lax.dot_general(k_st, vp, (((0,), (0,)), ((), ())), preferred_element_type=f32)
            state_ref[p] = cdec_ref[p] * state + jnp.where(head0_row, kv[:, :RET_V_DIM], kv[:, RET_V_DIM:])

    g_ret = proj(3)
    ret_parts = []
    for h in range(RET_HEADS):
        cols = slice(h * RET_V_DIM, (h + 1) * RET_V_DIM)
        y = ret_ref[:, cols]
        mu = jnp.mean(y, axis=-1, keepdims=True)
        d = y - mu
        var = jnp.mean(d * d, axis=-1, keepdims=True)
        yn = d * lax.rsqrt(var + LN_EPS) * retg_ref[:, cols]
        ret_parts.append((yn * _silu(g_ret[:, cols])).astype(bf16))
    ret_out = jnp.concatenate(ret_parts, axis=1)

    u = proj(4) * jax.nn.sigmoid(proj(5))
    ubuf_ref[HALO:HALO + tile, :] = u
    acc = jnp.zeros((tile, CONV_WIDTH), f32) + dwb_ref[...]
    for j in range(CONV_KERNEL):
        w_row = CONV_KERNEL - 1 - j
        acc = acc + ubuf_ref[HALO - j:HALO - j + tile, :] * dwk_ref[w_row:w_row + 1, :]
    ubuf_ref[0:HALO, :] = ubuf_ref[tile:tile + HALO, :]
    cv = _silu(_layernorm_rows(acc, clng_ref[...], clnb_ref[...]))
    cv = jnp.dot(cv.astype(bf16), wpw_ref[...], preferred_element_type=f32) + bpw_ref[...]
    conv_out = (cv * _silu(proj(6))).astype(bf16)

    hmix = (jnp.dot(ret_out, wout_ref[0:RET_WIDTH, :], preferred_element_type=f32)
            + jnp.dot(conv_out, wout_ref[RET_WIDTH:RET_WIDTH + CONV_WIDTH, :], preferred_element_type=f32))
    o_ref[...] = _layernorm_rows(DEEPNORM_ALPHA * x + hmix, plng_ref[...], plnb_ref[...])


def _const_spec(shape):
    zeros = (0,) * len(shape)
    return pl.BlockSpec(shape, lambda b, s: zeros, pipeline_mode=pl.Buffered(1))


def _layer(x, w_in, ret_norm_g, dw_kernel, dw_bias, conv_ln_g, conv_ln_b,
           w_pw2, b_pw2, w_out, post_ln_g, post_ln_b):
    batch, seq, d = x.shape
    assert d == D_MODEL and seq % SEQ_TILE == 0 and SEQ_TILE % CHUNK == 0
    tile = SEQ_TILE
    bf16 = jnp.bfloat16
    cos, sin = _rope_tables(seq)
    qdec, kdin, kdst, cdec = _decay_tables()
    row2 = lambda a: a.reshape(1, -1)

    args = (x, w_in.astype(bf16), w_pw2.astype(bf16), w_out.astype(bf16), cos, sin,
            qdec, kdin, kdst, cdec,
            row2(ret_norm_g), dw_kernel, row2(dw_bias), row2(conv_ln_g), row2(conv_ln_b),
            row2(b_pw2), row2(post_ln_g), row2(post_ln_b))
    in_specs = [
        pl.BlockSpec((None, tile, D_MODEL), lambda b, s: (b, s, 0)),
        _const_spec(args[1].shape), _const_spec(args[2].shape), _const_spec(args[3].shape),
        pl.BlockSpec((tile, LANES), lambda b, s: (s, 0)),
        pl.BlockSpec((tile, LANES), lambda b, s: (s, 0)),
    ] + [_const_spec(a.shape) for a in args[6:]]

    return pl.pallas_call(
        _layer_kernel,
        grid=(batch, seq // tile),
        in_specs=in_specs,
        out_specs=pl.BlockSpec((None, tile, D_MODEL), lambda b, s: (b, s, 0)),
        out_shape=jax.ShapeDtypeStruct(x.shape, x.dtype),
        scratch_shapes=[
            pltpu.VMEM((HEAD_PAIRS, LANES, RET_V_DIM), jnp.float32),
            pltpu.VMEM((HALO + tile, CONV_WIDTH), jnp.float32),
            pltpu.VMEM((tile, RET_WIDTH), jnp.float32),
        ],
        compiler_params=pltpu.CompilerParams(
            dimension_semantics=("arbitrary", "arbitrary"),
            vmem_limit_bytes=VMEM_LIMIT_BYTES),
        name="hybrid_layer",
    )(*args)


@jax.jit
def kernel(x, w_in, ret_norm_g, dw_kernel, dw_bias, conv_ln_g, conv_ln_b, w_pw2, b_pw2, w_out, post_ln_g, post_ln_b):
    for layer in range(DEPTH):
        x = _layer(x, w_in[layer], ret_norm_g[layer], dw_kernel[layer], dw_bias[layer],
                   conv_ln_g[layer], conv_ln_b[layer], w_pw2[layer], b_pw2[layer],
                   w_out[layer], post_ln_g[layer], post_ln_b[layer])
    return x
```

```python
import numpy as np
import jax
import jax.numpy as jnp
from jax import lax
from jax.experimental import pallas as pl
from jax.experimental.pallas import tpu as pltpu

D_MODEL = 1024
RET_HEADS = 8
RET_QK_DIM = 64
RET_V_DIM = 128
RET_QK_WIDTH = RET_HEADS * RET_QK_DIM
RET_WIDTH = RET_HEADS * RET_V_DIM
CHUNK = 128
ROPE_THETA = 10000.0
CONV_WIDTH = D_MODEL
CONV_KERNEL = 31
IN_WIDTH = 2 * RET_QK_WIDTH + 2 * RET_WIDTH + 3 * CONV_WIDTH
LN_EPS = 1e-5
DEPTH = 1
DEEPNORM_ALPHA = (2.0 * DEPTH) ** 0.25

LANES = 128
SUBLANES = 8
HEAD_PAIRS = RET_HEADS // 2
CONV_BLOCKS = CONV_WIDTH // LANES
HALO = 32
SEQ_TILE = 256
VMEM_LIMIT_BYTES = 56 * 1024 * 1024

PIECE = 256
PIECES_PER_ITER = 8
_SRC_OFF = np.cumsum([0, RET_QK_WIDTH, RET_QK_WIDTH, RET_WIDTH, RET_WIDTH, CONV_WIDTH, CONV_WIDTH, CONV_WIDTH])
_GROUP_ORDER = (0, 1, 2, 3, 6, 4, 5)
_PIECE_SRC = []
_FIRST_PIECE = {}
for _g in _GROUP_ORDER:
    _FIRST_PIECE[_g] = len(_PIECE_SRC)
    _PIECE_SRC += list(range(int(_SRC_OFF[_g]) // PIECE, int(_SRC_OFF[_g + 1]) // PIECE))
N_PIECES = IN_WIDTH // PIECE
Z_PIECES = _FIRST_PIECE[4]
Q0, K0, V0, GRET0, GCONV0, A0, GLU0 = (_FIRST_PIECE[g] for g in _GROUP_ORDER)


def _decay_tables():
    log_g = np.log1p(-np.exp2(-5.0 - np.arange(RET_HEADS, dtype=np.float64)))
    lane_head = np.arange(LANES) // RET_QK_DIM
    idx = np.arange(CHUNK, dtype=np.float64)
    scale = RET_QK_DIM ** -0.5
    qdec = np.zeros((HEAD_PAIRS, CHUNK, LANES))
    kdec_in = np.zeros((HEAD_PAIRS, CHUNK, LANES))
    kdec_st = np.zeros((HEAD_PAIRS, CHUNK, LANES))
    cdec = np.zeros((HEAD_PAIRS, LANES, RET_V_DIM))
    for p in range(HEAD_PAIRS):
        lg = log_g[2 * p + lane_head]
        qdec[p] = np.exp(lg[None, :] * (idx[:, None] + 1.0))
        kdec_in[p] = scale * np.exp(-lg[None, :] * (idx[:, None] + 1.0))
        kdec_st[p] = scale * np.exp(lg[None, :] * (CHUNK - 1.0 - idx[:, None]))
        cdec[p] = np.exp(lg * CHUNK)[:, None] * np.ones((1, RET_V_DIM))
    f = lambda a: jnp.asarray(a, jnp.float32)
    return f(qdec), f(kdec_in), f(kdec_st), f(cdec)


def _rope_tables(seq):
    half = RET_QK_DIM // 2
    inv_freq = ROPE_THETA ** (-np.arange(0, half, dtype=np.float64) * 2.0 / RET_QK_DIM)
    ang = np.arange(seq, dtype=np.float64)[:, None] * inv_freq[None, :]
    lane = np.arange(LANES)
    cos = np.cos(ang)[:, lane % half]
    sign = np.where((lane % RET_QK_DIM) < half, -1.0, 1.0)
    sin = np.sin(ang)[:, lane % half] * sign[None, :]
    return jnp.asarray(cos, jnp.float32), jnp.asarray(sin, jnp.float32)


def _silu(x):
    return x * jax.nn.sigmoid(x)


def _layernorm_rows(x, g, b):
    mu = jnp.mean(x, axis=-1, keepdims=True)
    d = x - mu
    var = jnp.mean(d * d, axis=-1, keepdims=True)
    return d * lax.rsqrt(var + LN_EPS) * g + b


def _layer_kernel(x_ref, win_ref, wpw_ref, wout_ref, cos_ref, sin_ref,
                  qdec_ref, kdin_ref, kdst_ref, cdec_ref,
                  retg_ref, dwk_ref, dwb_ref, clng_ref, clnb_ref, bpw_ref, plng_ref, plnb_ref,
                  o_ref, state_ref, ut_ref, cn_ref, ret_ref, z_ref, xb_ref):
    tile = x_ref.shape[0]
    groups = tile // SUBLANES
    f32, bf16 = jnp.float32, jnp.bfloat16

    @pl.when(pl.program_id(1) == 0)
    def _():
        state_ref[...] = jnp.zeros_like(state_ref)
        ut_ref[0:HALO * SUBLANES, :] = jnp.zeros((HALO * SUBLANES, LANES), f32)

    xb_ref[...] = x_ref[...].astype(bf16)

    def project(piece):
        return jnp.dot(xb_ref[...], win_ref[piece], preferred_element_type=f32)

    def zlanes(first_piece, block):
        per = PIECE // LANES
        return z_ref[first_piece + block // per, :, (block % per) * LANES:(block % per + 1) * LANES]

    for k in range(CONV_WIDTH // PIECE):
        u = project(A0 + k) * jax.nn.sigmoid(project(GLU0 + k))
        for c in range(PIECE // LANES):
            cb = k * (PIECE // LANES) + c
            for tg in range(groups):
                ut_ref[pl.ds((HALO + tg * SUBLANES) * SUBLANES + cb, SUBLANES, stride=SUBLANES), :] = (
                    u[tg * SUBLANES:(tg + 1) * SUBLANES, c * LANES:(c + 1) * LANES])

    span = SUBLANES + CONV_KERNEL - 1
    groups_per_piece = groups // Z_PIECES

    def conv_group(tg):
        base = pl.multiple_of((HALO + tg * SUBLANES - (CONV_KERNEL - 1)) * SUBLANES, SUBLANES)
        acc = [dwb_ref[...]] * SUBLANES
        for m in range(span):
            um = ut_ref[pl.ds(base + m * SUBLANES, SUBLANES), :]
            for i in range(SUBLANES):
                if 0 <= m - i < CONV_KERNEL:
                    acc[i] = acc[i] + um * dwk_ref[m - i]
        for i in range(SUBLANES):
            cn_ref[tg, pl.ds(i, SUBLANES, stride=SUBLANES), :] = acc[i]

    def conv_and_project(it, carry):
        for j in range(PIECES_PER_ITER):
            piece = it * PIECES_PER_ITER + j
            z_ref[piece] = project(piece)
            for g in range(groups_per_piece):
                conv_group(piece * groups_per_piece + g)
        return carry

    lax.fori_loop(0, Z_PIECES // PIECES_PER_ITER, conv_and_project, 0)

    ut_ref[0:HALO * SUBLANES, :] = ut_ref[tile * SUBLANES:(tile + HALO) * SUBLANES, :]
    cv = jnp.concatenate(
        [cn_ref[:, cb * SUBLANES:(cb + 1) * SUBLANES, :].reshape(tile, LANES) for cb in range(CONV_BLOCKS)], axis=1)
    cv = _silu(_layernorm_rows(cv, clng_ref[...], clnb_ref[...]))
    cv = jnp.dot(cv.astype(bf16), wpw_ref[...], preferred_element_type=f32) + bpw_ref[...]
    g_conv = jnp.concatenate([z_ref[GCONV0 + k] for k in range(CONV_WIDTH // PIECE)], axis=1)
    conv_out = (cv * _silu(g_conv)).astype(bf16)

    cos = cos_ref[...]
    sin = sin_ref[...]
    lane = lax.broadcasted_iota(jnp.int32, (tile, LANES), 1)
    first_half = (lane % RET_QK_DIM) < (RET_QK_DIM // 2)

    def rotary(blk):
        swapped = jnp.where(first_half,
                            pltpu.roll(blk, LANES - RET_QK_DIM // 2, axis=1),
                            pltpu.roll(blk, RET_QK_DIM // 2, axis=1))
        return blk * cos + swapped * sin

    q_blocks = [rotary(zlanes(Q0, p)) for p in range(HEAD_PAIRS)]
    k_blocks = [rotary(zlanes(K0, p)) for p in range(HEAD_PAIRS)]

    row = lax.broadcasted_iota(jnp.int32, (CHUNK, CHUNK), 0)
    col = lax.broadcasted_iota(jnp.int32, (CHUNK, CHUNK), 1)
    causal = row >= col
    head0_lane = col < RET_QK_DIM
    head0_row = row < RET_QK_DIM
    for c in range(tile // CHUNK):
        rows = slice(c * CHUNK, (c + 1) * CHUNK)
        for p in range(HEAD_PAIRS):
            qp = q_blocks[p][rows] * qdec_ref[p]
            kp = k_blocks[p][rows]
            k_in = (kp * kdin_ref[p]).astype(bf16)
            k_st = (kp * kdst_ref[p]).astype(bf16)
            vp = z_ref[V0 + p, rows, :].astype(bf16)
            state = state_ref[p]
            state_b = state.astype(bf16)
            for e in range(2):
                qm = jnp.where(head0_lane if e == 0 else ~head0_lane, qp, 0.0).astype(bf16)
                s = lax.dot_general(qm, k_in, (((1,), (1,)), ((), ())), preferred_element_type=f32)
                pm = jnp.where(causal, s, 0.0).astype(bf16)
                o = (jnp.dot(pm, vp[:, e * RET_V_DIM:(e + 1) * RET_V_DIM], preferred_element_type=f32)
                     + jnp.dot(qm, state_b, preferred_element_type=f32))
                h = 2 * p + e
                ret_ref[rows, h * RET_V_DIM:(h + 1) * RET_V_DIM] = o
            kv = lax.dot_general(k_st, vp, (((0,), (0,)), ((), ())), preferred_element_type=f32)
            state_ref[p] = cdec_ref[p] * state + jnp.where(head0_row, kv[:, :RET_V_DIM], kv[:, RET_V_DIM:])

    ret_parts = []
    for h in range(RET_HEADS):
        cols = slice(h * RET_V_DIM, (h + 1) * RET_V_DIM)
        y = ret_ref[:, cols]
        mu = jnp.mean(y, axis=-1, keepdims=True)
        d = y - mu
        var = jnp.mean(d * d, axis=-1, keepdims=True)
        yn = d * lax.rsqrt(var + LN_EPS) * retg_ref[:, cols]
        ret_parts.append((yn * _silu(zlanes(GRET0, h))).astype(bf16))
    ret_out = jnp.concatenate(ret_parts, axis=1)

    hmix = (jnp.dot(ret_out, wout_ref[0:RET_WIDTH, :], preferred_element_type=f32)
            + jnp.dot(conv_out, wout_ref[RET_WIDTH:RET_WIDTH + CONV_WIDTH, :], preferred_element_type=f32))
    o_ref[...] = _layernorm_rows(DEEPNORM_ALPHA * x_ref[...] + hmix, plng_ref[...], plnb_ref[...])


def _const_spec(shape):
    zeros = (0,) * len(shape)
    return pl.BlockSpec(shape, lambda b, s: zeros, pipeline_mode=pl.Buffered(1))


def _layer(x, w_in, ret_norm_g, dw_kernel, dw_bias, conv_ln_g, conv_ln_b,
           w_pw2, b_pw2, w_out, post_ln_g, post_ln_b):
    batch, seq, d = x.shape
    assert d == D_MODEL and seq % SEQ_TILE == 0 and SEQ_TILE % CHUNK == 0
    assert (SEQ_TILE // SUBLANES) % Z_PIECES == 0
    tile = SEQ_TILE
    bf16 = jnp.bfloat16
    cos, sin = _rope_tables(seq)
    qdec, kdin, kdst, cdec = _decay_tables()
    row2 = lambda a: a.reshape(1, -1)
    w_slabs = w_in.astype(bf16).reshape(D_MODEL, N_PIECES, PIECE).transpose(1, 0, 2)[np.asarray(_PIECE_SRC)]

    args = (x, w_slabs, w_pw2.astype(bf16), w_out.astype(bf16), cos, sin,
            qdec, kdin, kdst, cdec,
            row2(ret_norm_g), dw_kernel.reshape(CONV_KERNEL, CONV_BLOCKS, LANES),
            dw_bias.reshape(CONV_BLOCKS, LANES), row2(conv_ln_g), row2(conv_ln_b),
            row2(b_pw2), row2(post_ln_g), row2(post_ln_b))
    in_specs = [
        pl.BlockSpec((None, tile, D_MODEL), lambda b, s: (b, s, 0)),
        _const_spec(args[1].shape), _const_spec(args[2].shape), _const_spec(args[3].shape),
        pl.BlockSpec((tile, LANES), lambda b, s: (s, 0)),
        pl.BlockSpec((tile, LANES), lambda b, s: (s, 0)),
    ] + [_const_spec(a.shape) for a in args[6:]]

    return pl.pallas_call(
        _layer_kernel,
        grid=(batch, seq // tile),
        in_specs=in_specs,
        out_specs=pl.BlockSpec((None, tile, D_MODEL), lambda b, s: (b, s, 0)),
        out_shape=jax.ShapeDtypeStruct(x.shape, x.dtype),
        scratch_shapes=[
            pltpu.VMEM((HEAD_PAIRS, LANES, RET_V_DIM), jnp.float32),
            pltpu.VMEM(((HALO + tile) * SUBLANES, LANES), jnp.float32),
            pltpu.VMEM((tile // SUBLANES, CONV_BLOCKS * SUBLANES, LANES), jnp.float32),
            pltpu.VMEM((tile, RET_WIDTH), jnp.float32),
            pltpu.VMEM((Z_PIECES, tile, PIECE), jnp.float32),
            pltpu.VMEM((tile, D_MODEL), jnp.bfloat16),
        ],
        compiler_params=pltpu.CompilerParams(
            dimension_semantics=("arbitrary", "arbitrary"),
            vmem_limit_bytes=VMEM_LIMIT_BYTES),
        name="hybrid_layer",
    )(*args)


@jax.jit
def kernel(x, w_in, ret_norm_g, dw_kernel, dw_bias, conv_ln_g, conv_ln_b, w_pw2, b_pw2, w_out, post_ln_g, post_ln_b):
    for layer in range(DEPTH):
        x = _layer(x, w_in[layer], ret_norm_g[layer], dw_kernel[layer], dw_bias[layer],
                   conv_ln_g[layer], conv_ln_b[layer], w_pw2[layer], b_pw2[layer],
                   w_out[layer], post_ln_g[layer], post_ln_b[layer])
    return x
```

```python
import numpy as np
import jax
import jax.numpy as jnp
from jax import lax
from jax.experimental import pallas as pl
from jax.experimental.pallas import tpu as pltpu

D_MODEL = 1024
RET_HEADS = 8
RET_QK_DIM = 64
RET_V_DIM = 128
RET_QK_WIDTH = RET_HEADS * RET_QK_DIM
RET_WIDTH = RET_HEADS * RET_V_DIM
CHUNK = 128
ROPE_THETA = 10000.0
CONV_WIDTH = D_MODEL
CONV_KERNEL = 31
LN_EPS = 1e-5
DEPTH = 1
DEEPNORM_ALPHA = (2.0 * DEPTH) ** 0.25

LANES = 128
SUBLANES = 8
HEAD_PAIRS = RET_HEADS // 2
CONV_BLOCKS = CONV_WIDTH // LANES
HALO = 32
SEQ_TILE = 256
VMEM_LIMIT_BYTES = 56 * 1024 * 1024

_OFF = [int(v) for v in np.cumsum([0, RET_QK_WIDTH, RET_QK_WIDTH, RET_WIDTH, RET_WIDTH,
                                   CONV_WIDTH, CONV_WIDTH, CONV_WIDTH])]
Q, K, V, G_RET, A_CONV, GLU_CONV, G_CONV = range(7)


def _decay_tables():
    log_g = np.log1p(-np.exp2(-5.0 - np.arange(RET_HEADS, dtype=np.float64)))
    lane_head = np.arange(LANES) // RET_QK_DIM
    idx = np.arange(CHUNK, dtype=np.float64)
    scale = RET_QK_DIM ** -0.5
    qdec = np.zeros((HEAD_PAIRS, CHUNK, LANES))
    kdec_in = np.zeros((HEAD_PAIRS, CHUNK, LANES))
    kdec_st = np.zeros((HEAD_PAIRS, CHUNK, LANES))
    cdec = np.zeros((HEAD_PAIRS, LANES, RET_V_DIM))
    for p in range(HEAD_PAIRS):
        lg = log_g[2 * p + lane_head]
        qdec[p] = np.exp(lg[None, :] * (idx[:, None] + 1.0))
        kdec_in[p] = scale * np.exp(-lg[None, :] * (idx[:, None] + 1.0))
        kdec_st[p] = scale * np.exp(lg[None, :] * (CHUNK - 1.0 - idx[:, None]))
        cdec[p] = np.exp(lg * CHUNK)[:, None] * np.ones((1, RET_V_DIM))
    f = lambda a: jnp.asarray(a, jnp.float32)
    return f(qdec), f(kdec_in), f(kdec_st), f(cdec)


def _rope_tables(seq):
    half = RET_QK_DIM // 2
    inv_freq = ROPE_THETA ** (-np.arange(0, half, dtype=np.float64) * 2.0 / RET_QK_DIM)
    ang = np.arange(seq, dtype=np.float64)[:, None] * inv_freq[None, :]
    lane = np.arange(LANES)
    cos = np.cos(ang)[:, lane % half]
    sign = np.where((lane % RET_QK_DIM) < half, -1.0, 1.0)
    sin = np.sin(ang)[:, lane % half] * sign[None, :]
    return jnp.asarray(cos, jnp.float32), jnp.asarray(sin, jnp.float32)


def _silu(x):
    return x * jax.nn.sigmoid(x)


def _layernorm_rows(x, g, b):
    mu = jnp.mean(x, axis=-1, keepdims=True)
    d = x - mu
    var = jnp.mean(d * d, axis=-1, keepdims=True)
    return d * lax.rsqrt(var + LN_EPS) * g + b


def _layer_kernel(x_ref, win_ref, wpw_ref, wout_ref, cos_ref, sin_ref,
                  qdec_ref, kdin_ref, kdst_ref, cdec_ref,
                  retg_ref, dwk_ref, dwb_ref, clng_ref, clnb_ref, bpw_ref, plng_ref, plnb_ref,
                  o_ref, state_ref, ut_ref, cn_ref, ret_ref):
    tile = x_ref.shape[0]
    groups = tile // SUBLANES
    f32, bf16 = jnp.float32, jnp.bfloat16

    @pl.when(pl.program_id(1) == 0)
    def _():
        state_ref[...] = jnp.zeros_like(state_ref)
        ut_ref[0:HALO * SUBLANES, :] = jnp.zeros((HALO * SUBLANES, LANES), f32)

    xb = x_ref[...].astype(bf16)

    def proj(group):
        return jnp.dot(xb, win_ref[:, _OFF[group]:_OFF[group + 1]], preferred_element_type=f32)

    u = proj(A_CONV) * jax.nn.sigmoid(proj(GLU_CONV))
    for cb in range(CONV_BLOCKS):
        for tg in range(groups):
            ut_ref[pl.ds((HALO + tg * SUBLANES) * SUBLANES + cb, SUBLANES, stride=SUBLANES), :] = (
                u[tg * SUBLANES:(tg + 1) * SUBLANES, cb * LANES:(cb + 1) * LANES])
    taps = [dwk_ref[j] for j in range(CONV_KERNEL)]
    bias = dwb_ref[...]
    span = SUBLANES + CONV_KERNEL - 1
    for tg in range(groups):
        acc = [bias] * SUBLANES
        for m in range(span):
            um = ut_ref[pl.ds((HALO + tg * SUBLANES - (CONV_KERNEL - 1) + m) * SUBLANES, SUBLANES), :]
            for i in range(SUBLANES):
                if 0 <= m - i < CONV_KERNEL:
                    acc[i] = acc[i] + um * taps[m - i]
        for i in range(SUBLANES):
            cn_ref[tg, pl.ds(i, SUBLANES, stride=SUBLANES), :] = acc[i]
    ut_ref[0:HALO * SUBLANES, :] = ut_ref[tile * SUBLANES:(tile + HALO) * SUBLANES, :]
    cv = jnp.concatenate(
        [cn_ref[:, cb * SUBLANES:(cb + 1) * SUBLANES, :].reshape(tile, LANES) for cb in range(CONV_BLOCKS)], axis=1)
    cv = _silu(_layernorm_rows(cv, clng_ref[...], clnb_ref[...]))
    cv = jnp.dot(cv.astype(bf16), wpw_ref[...], preferred_element_type=f32) + bpw_ref[...]
    conv_out = (cv * _silu(proj(G_CONV))).astype(bf16)

    cos = cos_ref[...]
    sin = sin_ref[...]
    lane = lax.broadcasted_iota(jnp.int32, (tile, LANES), 1)
    first_half = (lane % RET_QK_DIM) < (RET_QK_DIM // 2)

    def rotary(t):
        blocks = []
        for p in range(HEAD_PAIRS):
            blk = t[:, p * LANES:(p + 1) * LANES]
            swapped = jnp.where(first_half,
                                pltpu.roll(blk, LANES - RET_QK_DIM // 2, axis=1),
                                pltpu.roll(blk, RET_QK_DIM // 2, axis=1))
            blocks.append(blk * cos + swapped * sin)
        return blocks

    q_blocks = rotary(proj(Q))
    k_blocks = rotary(proj(K))
    v = proj(V).astype(bf16)

    row = lax.broadcasted_iota(jnp.int32, (CHUNK, CHUNK), 0)
    col = lax.broadcasted_iota(jnp.int32, (CHUNK, CHUNK), 1)
    causal = row >= col
    head0_lane = col < RET_QK_DIM
    head0_row = row < RET_QK_DIM
    for c in range(tile // CHUNK):
        rows = slice(c * CHUNK, (c + 1) * CHUNK)
        for p in range(HEAD_PAIRS):
            qp = q_blocks[p][rows] * qdec_ref[p]
            kp = k_blocks[p][rows]
            k_in = (kp * kdin_ref[p]).astype(bf16)
            k_st = (kp * kdst_ref[p]).astype(bf16)
            vp = v[rows, 2 * p * RET_V_DIM:(2 * p + 2) * RET_V_DIM]
            state = state_ref[p]
            state_b = state.astype(bf16)
            for e in range(2):
                qm = jnp.where(head0_lane if e == 0 else ~head0_lane, qp, 0.0).astype(bf16)
                s = lax.dot_general(qm, k_in, (((1,), (1,)), ((), ())), preferred_element_type=f32)
                pm = jnp.where(causal, s, 0.0).astype(bf16)
                o = (jnp.dot(pm, vp[:, e * RET_V_DIM:(e + 1) * RET_V_DIM], preferred_element_type=f32)
                     + jnp.dot(qm, state_b, preferred_element_type=f32))
                h = 2 * p + e
                ret_ref[rows, h * RET_V_DIM:(h + 1) * RET_V_DIM] = o
            kv = lax.dot_general(k_st, vp, (((0,), (0,)), ((), ())), preferred_element_type=f32)
            state_ref[p] = cdec_ref[p] * state + jnp.where(head0_row, kv[:, :RET_V_DIM], kv[:, RET_V_DIM:])

    g_ret = proj(G_RET)
    ret_parts = []
    for h in range(RET_HEADS):
        cols = slice(h * RET_V_DIM, (h + 1) * RET_V_DIM)
        y = ret_ref[:, cols]
        mu = jnp.mean(y, axis=-1, keepdims=True)
        d = y - mu
        var = jnp.mean(d * d, axis=-1, keepdims=True)
        yn = d * lax.rsqrt(var + LN_EPS) * retg_ref[:, cols]
        ret_parts.append((yn * _silu(g_ret[:, cols])).astype(bf16))
    ret_out = jnp.concatenate(ret_parts, axis=1)

    hmix = (jnp.dot(ret_out, wout_ref[0:RET_WIDTH, :], preferred_element_type=f32)
            + jnp.dot(conv_out, wout_ref[RET_WIDTH:RET_WIDTH + CONV_WIDTH, :], preferred_element_type=f32))
    o_ref[...] = _layernorm_rows(DEEPNORM_ALPHA * x_ref[...] + hmix, plng_ref[...], plnb_ref[...])


def _const_spec(shape):
    zeros = (0,) * len(shape)
    return pl.BlockSpec(shape, lambda b, s: zeros, pipeline_mode=pl.Buffered(1))


def _layer(x, w_in, ret_norm_g, dw_kernel, dw_bias, conv_ln_g, conv_ln_b,
           w_pw2, b_pw2, w_out, post_ln_g, post_ln_b):
    batch, seq, d = x.shape
    tile = SEQ_TILE
    assert d == D_MODEL and seq % tile == 0 and tile % CHUNK == 0
    bf16 = jnp.bfloat16
    cos, sin = _rope_tables(seq)
    qdec, kdin, kdst, cdec = _decay_tables()
    row2 = lambda a: a.reshape(1, -1)

    args = (x, w_in.astype(bf16), w_pw2.astype(bf16), w_out.astype(bf16), cos, sin,
            qdec, kdin, kdst, cdec,
            row2(ret_norm_g), dw_kernel.reshape(CONV_KERNEL, CONV_BLOCKS, LANES),
            dw_bias.reshape(CONV_BLOCKS, LANES), row2(conv_ln_g), row2(conv_ln_b),
            row2(b_pw2), row2(post_ln_g), row2(post_ln_b))
    in_specs = [
        pl.BlockSpec((None, tile, D_MODEL), lambda b, s: (b, s, 0)),
        _const_spec(args[1].shape), _const_spec(args[2].shape), _const_spec(args[3].shape),
        pl.BlockSpec((tile, LANES), lambda b, s: (s, 0)),
        pl.BlockSpec((tile, LANES), lambda b, s: (s, 0)),
    ] + [_const_spec(a.shape) for a in args[6:]]

    return pl.pallas_call(
        _layer_kernel,
        grid=(batch, seq // tile),
        in_specs=in_specs,
        out_specs=pl.BlockSpec((None, tile, D_MODEL), lambda b, s: (b, s, 0)),
        out_shape=jax.ShapeDtypeStruct(x.shape, x.dtype),
        scratch_shapes=[
            pltpu.VMEM((HEAD_PAIRS, LANES, RET_V_DIM), jnp.float32),
            pltpu.VMEM(((HALO + tile) * SUBLANES, LANES), jnp.float32),
            pltpu.VMEM((tile // SUBLANES, CONV_BLOCKS * SUBLANES, LANES), jnp.float32),
            pltpu.VMEM((tile, RET_WIDTH), jnp.float32),
        ],
        compiler_params=pltpu.CompilerParams(
            dimension_semantics=("arbitrary", "arbitrary"),
            vmem_limit_bytes=VMEM_LIMIT_BYTES),
        name="hybrid_layer",
    )(*args)


@jax.jit
def kernel(x, w_in, ret_norm_g, dw_kernel, dw_bias, conv_ln_g, conv_ln_b, w_pw2, b_pw2, w_out, post_ln_g, post_ln_b):
    for layer in range(DEPTH):
        x = _layer(x, w_in[layer], ret_norm_g[layer], dw_kernel[layer], dw_bias[layer],
                   conv_ln_g[layer], conv_ln_b[layer], w_pw2[layer], b_pw2[layer],
                   w_out[layer], post_ln_g[layer], post_ln_b[layer])
    return x
```

```python
import numpy as np
import jax
import jax.numpy as jnp
from jax import lax
from jax.experimental import pallas as pl
from jax.experimental.pallas import tpu as pltpu

D_MODEL = 1024
RET_HEADS = 8
RET_QK_DIM = 64
RET_V_DIM = 128
RET_QK_WIDTH = RET_HEADS * RET_QK_DIM
RET_WIDTH = RET_HEADS * RET_V_DIM
CHUNK = 128
ROPE_THETA = 10000.0
CONV_WIDTH = D_MODEL
CONV_KERNEL = 31
LN_EPS = 1e-5
DEPTH = 1
DEEPNORM_ALPHA = (2.0 * DEPTH) ** 0.25

LANES = 128
SUBLANES = 8
HEAD_PAIRS = RET_HEADS // 2
CONV_BLOCKS = CONV_WIDTH // LANES
HALO = 32
SEQ_TILE = 512
VMEM_LIMIT_BYTES = 56 * 1024 * 1024

_OFF = [int(v) for v in np.cumsum([0, RET_QK_WIDTH, RET_QK_WIDTH, RET_WIDTH, RET_WIDTH,
                                   CONV_WIDTH, CONV_WIDTH, CONV_WIDTH])]
Q, K, V, G_RET, A_CONV, GLU_CONV, G_CONV = range(7)


def _decay_tables():
    log_g = np.log1p(-np.exp2(-5.0 - np.arange(RET_HEADS, dtype=np.float64)))
    lane_head = np.arange(LANES) // RET_QK_DIM
    idx = np.arange(CHUNK, dtype=np.float64)
    scale = RET_QK_DIM ** -0.5
    qdec = np.zeros((HEAD_PAIRS, CHUNK, LANES))
    kdec_in = np.zeros((HEAD_PAIRS, CHUNK, LANES))
    kdec_st = np.zeros((HEAD_PAIRS, CHUNK, LANES))
    cdec = np.zeros((HEAD_PAIRS, LANES, RET_V_DIM))
    for p in range(HEAD_PAIRS):
        lg = log_g[2 * p + lane_head]
        qdec[p] = np.exp(lg[None, :] * (idx[:, None] + 1.0))
        kdec_in[p] = scale * np.exp(-lg[None, :] * (idx[:, None] + 1.0))
        kdec_st[p] = scale * np.exp(lg[None, :] * (CHUNK - 1.0 - idx[:, None]))
        cdec[p] = np.exp(lg * CHUNK)[:, None] * np.ones((1, RET_V_DIM))
    f = lambda a: jnp.asarray(a, jnp.float32)
    return f(qdec), f(kdec_in), f(kdec_st), f(cdec)


def _rope_tables(seq):
    half = RET_QK_DIM // 2
    inv_freq = ROPE_THETA ** (-np.arange(0, half, dtype=np.float64) * 2.0 / RET_QK_DIM)
    ang = np.arange(seq, dtype=np.float64)[:, None] * inv_freq[None, :]
    lane = np.arange(LANES)
    cos = np.cos(ang)[:, lane % half]
    sign = np.where((lane % RET_QK_DIM) < half, -1.0, 1.0)
    sin = np.sin(ang)[:, lane % half] * sign[None, :]
    return jnp.asarray(cos, jnp.float32), jnp.asarray(sin, jnp.float32)


def _silu(x):
    return x * jax.nn.sigmoid(x)


def _layernorm_rows(x, g, b):
    mu = jnp.mean(x, axis=-1, keepdims=True)
    d = x - mu
    var = jnp.mean(d * d, axis=-1, keepdims=True)
    return d * lax.rsqrt(var + LN_EPS) * g + b


def _layer_kernel(x_ref, win_ref, wpw_ref, wout_ref, cos_ref, sin_ref,
                  qdec_ref, kdin_ref, kdst_ref, cdec_ref,
                  retg_ref, dwk_ref, dwb_ref, clng_ref, clnb_ref, bpw_ref, plng_ref, plnb_ref,
                  o_ref, state_ref, ut_ref, cn_ref, ret_ref):
    tile = x_ref.shape[0]
    groups = tile // SUBLANES
    f32, bf16 = jnp.float32, jnp.bfloat16

    @pl.when(pl.program_id(1) == 0)
    def _():
        state_ref[...] = jnp.zeros_like(state_ref)
        ut_ref[0:HALO * SUBLANES, :] = jnp.zeros((HALO * SUBLANES, LANES), f32)

    xb = x_ref[...].astype(bf16)

    def proj(group):
        return jnp.dot(xb, win_ref[:, _OFF[group]:_OFF[group + 1]], preferred_element_type=f32)

    u = proj(A_CONV) * jax.nn.sigmoid(proj(GLU_CONV))
    for cb in range(CONV_BLOCKS):
        for tg in range(groups):
            ut_ref[pl.ds((HALO + tg * SUBLANES) * SUBLANES + cb, SUBLANES, stride=SUBLANES), :] = (
                u[tg * SUBLANES:(tg + 1) * SUBLANES, cb * LANES:(cb + 1) * LANES])
    taps = [dwk_ref[j] for j in range(CONV_KERNEL)]
    bias = dwb_ref[...]
    span = SUBLANES + CONV_KERNEL - 1
    for tg in range(groups):
        acc = [bias] * SUBLANES
        for m in range(span):
            um = ut_ref[pl.ds((HALO + tg * SUBLANES - (CONV_KERNEL - 1) + m) * SUBLANES, SUBLANES), :]
            for i in range(SUBLANES):
                if 0 <= m - i < CONV_KERNEL:
                    acc[i] = acc[i] + um * taps[m - i]
        for i in range(SUBLANES):
            cn_ref[tg, pl.ds(i, SUBLANES, stride=SUBLANES), :] = acc[i]
    ut_ref[0:HALO * SUBLANES, :] = ut_ref[tile * SUBLANES:(tile + HALO) * SUBLANES, :]
    cv = jnp.concatenate(
        [cn_ref[:, cb * SUBLANES:(cb + 1) * SUBLANES, :].reshape(tile, LANES) for cb in range(CONV_BLOCKS)], axis=1)
    cv = _silu(_layernorm_rows(cv, clng_ref[...], clnb_ref[...]))
    cv = jnp.dot(cv.astype(bf16), wpw_ref[...], preferred_element_type=f32) + bpw_ref[...]
    conv_out = (cv * _silu(proj(G_CONV))).astype(bf16)

    cos = cos_ref[...]
    sin = sin_ref[...]
    lane = lax.broadcasted_iota(jnp.int32, (tile, LANES), 1)
    first_half = (lane % RET_QK_DIM) < (RET_QK_DIM // 2)

    def rotary(t):
        blocks = []
        for p in range(HEAD_PAIRS):
            blk = t[:, p * LANES:(p + 1) * LANES]
            swapped = jnp.where(first_half,
                                pltpu.roll(blk, LANES - RET_QK_DIM // 2, axis=1),
                                pltpu.roll(blk, RET_QK_DIM // 2, axis=1))
            blocks.append(blk * cos + swapped * sin)
        return blocks

    q_blocks = rotary(proj(Q))
    k_blocks = rotary(proj(K))
    v = proj(V).astype(bf16)

    row = lax.broadcasted_iota(jnp.int32, (CHUNK, CHUNK), 0)
    col = lax.broadcasted_iota(jnp.int32, (CHUNK, CHUNK), 1)
    head0_lane = col < RET_QK_DIM
    head0_row = row < RET_QK_DIM
    row2 = lax.broadcasted_iota(jnp.int32, (CHUNK, 2 * CHUNK), 0)
    col2 = lax.broadcasted_iota(jnp.int32, (CHUNK, 2 * CHUNK), 1)
    causal2 = row2 >= (col2 % CHUNK)
    head0_col = col2 < RET_V_DIM
    for c in range(tile // CHUNK):
        rows = slice(c * CHUNK, (c + 1) * CHUNK)
        for p in range(HEAD_PAIRS):
            qb = (q_blocks[p][rows] * qdec_ref[p]).astype(bf16)
            kp = k_blocks[p][rows]
            k_in = kp * kdin_ref[p]
            k_st = (kp * kdst_ref[p]).astype(bf16)
            k_heads = jnp.concatenate([jnp.where(head0_lane, k_in, 0.0),
                                       jnp.where(head0_lane, 0.0, k_in)], axis=0).astype(bf16)
            vp = v[rows, 2 * p * RET_V_DIM:(2 * p + 2) * RET_V_DIM]
            v_diag = jnp.concatenate([jnp.where(head0_col, vp, jnp.zeros_like(vp)),
                                      jnp.where(head0_col, jnp.zeros_like(vp), vp)], axis=0)
            state = state_ref[p]
            st_heads = jnp.concatenate([jnp.where(head0_row, state, 0.0),
                                        jnp.where(head0_row, 0.0, state)], axis=1).astype(bf16)
            s = lax.dot_general(qb, k_heads, (((1,), (1,)), ((), ())), preferred_element_type=f32)
            pm = jnp.where(causal2, s, 0.0).astype(bf16)
            o = (jnp.dot(pm, v_diag, preferred_element_type=f32)
                 + jnp.dot(qb, st_heads, preferred_element_type=f32))
            ret_ref[rows, 2 * p * RET_V_DIM:(2 * p + 2) * RET_V_DIM] = o
            kv = lax.dot_general(k_st, vp, (((0,), (0,)), ((), ())), preferred_element_type=f32)
            state_ref[p] = cdec_ref[p] * state + jnp.where(head0_row, kv[:, :RET_V_DIM], kv[:, RET_V_DIM:])

    g_ret = proj(G_RET)
    ret_parts = []
    for h in range(RET_HEADS):
        cols = slice(h * RET_V_DIM, (h + 1) * RET_V_DIM)
        y = ret_ref[:, cols]
        mu = jnp.mean(y, axis=-1, keepdims=True)
        d = y - mu
        var = jnp.mean(d * d, axis=-1, keepdims=True)
        yn = d * lax.rsqrt(var + LN_EPS) * retg_ref[:, cols]
        ret_parts.append((yn * _silu(g_ret[:, cols])).astype(bf16))
    ret_out = jnp.concatenate(ret_parts, axis=1)

    hmix = (jnp.dot(ret_out, wout_ref[0:RET_WIDTH, :], preferred_element_type=f32)
            + jnp.dot(conv_out, wout_ref[RET_WIDTH:RET_WIDTH + CONV_WIDTH, :], preferred_element_type=f32))
    o_ref[...] = _layernorm_rows(DEEPNORM_ALPHA * x_ref[...] + hmix, plng_ref[...], plnb_ref[...])


def _const_spec(shape):
    zeros = (0,) * len(shape)
    return pl.BlockSpec(shape, lambda b, s: zeros, pipeline_mode=pl.Buffered(1))


def _layer(x, w_in, ret_norm_g, dw_kernel, dw_bias, conv_ln_g, conv_ln_b,
           w_pw2, b_pw2, w_out, post_ln_g, post_ln_b):
    batch, seq, d = x.shape
    tile = SEQ_TILE
    assert d == D_MODEL and seq % tile == 0 and tile % CHUNK == 0
    bf16 = jnp.bfloat16
    cos, sin = _rope_tables(seq)
    qdec, kdin, kdst, cdec = _decay_tables()
    row2 = lambda a: a.reshape(1, -1)

    args = (x, w_in.astype(bf16), w_pw2.astype(bf16), w_out.astype(bf16), cos, sin,
            qdec, kdin, kdst, cdec,
            row2(ret_norm_g), dw_kernel.reshape(CONV_KERNEL, CONV_BLOCKS, LANES),
            dw_bias.reshape(CONV_BLOCKS, LANES), row2(conv_ln_g), row2(conv_ln_b),
            row2(b_pw2), row2(post_ln_g), row2(post_ln_b))
    in_specs = [
        pl.BlockSpec((None, tile, D_MODEL), lambda b, s: (b, s, 0)),
        _const_spec(args[1].shape), _const_spec(args[2].shape), _const_spec(args[3].shape),
        pl.BlockSpec((tile, LANES), lambda b, s: (s, 0)),
        pl.BlockSpec((tile, LANES), lambda b, s: (s, 0)),
    ] + [_const_spec(a.shape) for a in args[6:]]

    return pl.pallas_call(
        _layer_kernel,
        grid=(batch, seq // tile),
        in_specs=in_specs,
        out_specs=pl.BlockSpec((None, tile, D_MODEL), lambda b, s: (b, s, 0)),
        out_shape=jax.ShapeDtypeStruct(x.shape, x.dtype),
        scratch_shapes=[
            pltpu.VMEM((HEAD_PAIRS, LANES, RET_V_DIM), jnp.float32),
            pltpu.VMEM(((HALO + tile) * SUBLANES, LANES), jnp.float32),
            pltpu.VMEM((tile // SUBLANES, CONV_BLOCKS * SUBLANES, LANES), jnp.float32),
            pltpu.VMEM((tile, RET_WIDTH), jnp.float32),
        ],
        compiler_params=pltpu.CompilerParams(
            dimension_semantics=("arbitrary", "arbitrary"),
            vmem_limit_bytes=VMEM_LIMIT_BYTES),
        name="hybrid_layer",
    )(*args)


@jax.jit
def kernel(x, w_in, ret_norm_g, dw_kernel, dw_bias, conv_ln_g, conv_ln_b, w_pw2, b_pw2, w_out, post_ln_g, post_ln_b):
    for layer in range(DEPTH):
        x = _layer(x, w_in[layer], ret_norm_g[layer], dw_kernel[layer], dw_bias[layer],
                   conv_ln_g[layer], conv_ln_b[layer], w_pw2[layer], b_pw2[layer],
                   w_out[layer], post_ln_g[layer], post_ln_b[layer])
    return x
```

```python
import numpy as np
import jax
import jax.numpy as jnp
from jax import lax
from jax.experimental import pallas as pl
from jax.experimental.pallas import tpu as pltpu

D_MODEL = 1024
RET_HEADS = 8
RET_QK_DIM = 64
RET_V_DIM = 128
RET_QK_WIDTH = RET_HEADS * RET_QK_DIM
RET_WIDTH = RET_HEADS * RET_V_DIM
CHUNK = 128
ROPE_THETA = 10000.0
CONV_WIDTH = D_MODEL
CONV_KERNEL = 31
LN_EPS = 1e-5
DEPTH = 1
DEEPNORM_ALPHA = (2.0 * DEPTH) ** 0.25

LANES = 128
SUBLANES = 8
HEAD_PAIRS = RET_HEADS // 2
CONV_BLOCKS = CONV_WIDTH // LANES
HALO = 32
SEQ_TILE = 512
VMEM_LIMIT_BYTES = 56 * 1024 * 1024
STAGE_SLOTS = 4
WIDE_STAGE_ROWS = 64
NARROW_STAGE_ROWS = 256

_OFF = [int(v) for v in np.cumsum([0, RET_QK_WIDTH, RET_QK_WIDTH, RET_WIDTH, RET_WIDTH,
                                   CONV_WIDTH, CONV_WIDTH, CONV_WIDTH])]
Q, K, V, G_RET, A_CONV, GLU_CONV, G_CONV = range(7)


def _decay_tables():
    log_g = np.log1p(-np.exp2(-5.0 - np.arange(RET_HEADS, dtype=np.float64)))
    lane_head = np.arange(LANES) // RET_QK_DIM
    idx = np.arange(CHUNK, dtype=np.float64)
    scale = RET_QK_DIM ** -0.5
    qdec = np.zeros((HEAD_PAIRS, CHUNK, LANES))
    kdec_in = np.zeros((HEAD_PAIRS, CHUNK, LANES))
    kdec_st = np.zeros((HEAD_PAIRS, CHUNK, LANES))
    cdec = np.zeros((HEAD_PAIRS, LANES, RET_V_DIM))
    for p in range(HEAD_PAIRS):
        lg = log_g[2 * p + lane_head]
        qdec[p] = np.exp(lg[None, :] * (idx[:, None] + 1.0))
        kdec_in[p] = scale * np.exp(-lg[None, :] * (idx[:, None] + 1.0))
        kdec_st[p] = scale * np.exp(lg[None, :] * (CHUNK - 1.0 - idx[:, None]))
        cdec[p] = np.exp(lg * CHUNK)[:, None] * np.ones((1, RET_V_DIM))
    f = lambda a: jnp.asarray(a, jnp.float32)
    return f(qdec), f(kdec_in), f(kdec_st), f(cdec)


def _rope_tables(seq):
    half = RET_QK_DIM // 2
    inv_freq = ROPE_THETA ** (-np.arange(0, half, dtype=np.float64) * 2.0 / RET_QK_DIM)
    ang = np.arange(seq, dtype=np.float64)[:, None] * inv_freq[None, :]
    lane = np.arange(LANES)
    cos = np.cos(ang)[:, lane % half]
    sign = np.where((lane % RET_QK_DIM) < half, -1.0, 1.0)
    sin = np.sin(ang)[:, lane % half] * sign[None, :]
    return jnp.asarray(cos, jnp.float32), jnp.asarray(sin, jnp.float32)


def _silu(x):
    return x * jax.nn.sigmoid(x)


def _layernorm_rows(x, g, b):
    mu = jnp.mean(x, axis=-1, keepdims=True)
    d = x - mu
    var = jnp.mean(d * d, axis=-1, keepdims=True)
    return d * lax.rsqrt(var + LN_EPS) * g + b


def _load_as_bf16(src_hbm, dst_ref, stage_ref, sems):
    n_slots, chunk = stage_ref.shape[0], stage_ref.shape[1]
    n_chunks = src_hbm.shape[0] // chunk

    def chunk_copy(i):
        slot = i % n_slots
        return pltpu.make_async_copy(src_hbm.at[pl.ds(i * chunk, chunk)], stage_ref.at[slot], sems.at[slot])

    for i in range(min(n_slots - 1, n_chunks)):
        chunk_copy(i).start()
    for i in range(n_chunks):
        ahead = i + n_slots - 1
        if ahead < n_chunks:
            chunk_copy(ahead).start()
        chunk_copy(i).wait()
        dst_ref[pl.ds(i * chunk, chunk), :] = stage_ref[i % n_slots].astype(jnp.bfloat16)


def _layer_kernel(x_ref, win_hbm, wpw_hbm, wout_hbm, cos_ref, sin_ref,
                  qdec_ref, kdin_ref, kdst_ref, cdec_ref,
                  retg_ref, dwk_ref, dwb_ref, clng_ref, clnb_ref, bpw_ref, plng_ref, plnb_ref,
                  o_ref, state_ref, ut_ref, cn_ref, ret_ref,
                  win_ref, wpw_ref, wout_ref, wide_stage, narrow_stage, wide_sems, narrow_sems):
    tile = x_ref.shape[0]
    groups = tile // SUBLANES
    f32, bf16 = jnp.float32, jnp.bfloat16

    @pl.when((pl.program_id(0) == 0) & (pl.program_id(1) == 0))
    def _():
        _load_as_bf16(win_hbm, win_ref, wide_stage, wide_sems)
        _load_as_bf16(wpw_hbm, wpw_ref, narrow_stage, narrow_sems)
        _load_as_bf16(wout_hbm, wout_ref, narrow_stage, narrow_sems)

    @pl.when(pl.program_id(1) == 0)
    def _():
        state_ref[...] = jnp.zeros_like(state_ref)
        ut_ref[0:HALO * SUBLANES, :] = jnp.zeros((HALO * SUBLANES, LANES), f32)

    xb = x_ref[...].astype(bf16)

    def proj(group):
        return jnp.dot(xb, win_ref[:, _OFF[group]:_OFF[group + 1]], preferred_element_type=f32)

    u = proj(A_CONV) * jax.nn.sigmoid(proj(GLU_CONV))
    for cb in range(CONV_BLOCKS):
        for tg in range(groups):
            ut_ref[pl.ds((HALO + tg * SUBLANES) * SUBLANES + cb, SUBLANES, stride=SUBLANES), :] = (
                u[tg * SUBLANES:(tg + 1) * SUBLANES, cb * LANES:(cb + 1) * LANES])
    taps = [dwk_ref[j] for j in range(CONV_KERNEL)]
    bias = dwb_ref[...]
    span = SUBLANES + CONV_KERNEL - 1
    for tg in range(groups):
        acc = [bias] * SUBLANES
        for m in range(span):
            um = ut_ref[pl.ds((HALO + tg * SUBLANES - (CONV_KERNEL - 1) + m) * SUBLANES, SUBLANES), :]
            for i in range(SUBLANES):
                if 0 <= m - i < CONV_KERNEL:
                    acc[i] = acc[i] + um * taps[m - i]
        for i in range(SUBLANES):
            cn_ref[tg, pl.ds(i, SUBLANES, stride=SUBLANES), :] = acc[i]
    ut_ref[0:HALO * SUBLANES, :] = ut_ref[tile * SUBLANES:(tile + HALO) * SUBLANES, :]
    cv = jnp.concatenate(
        [cn_ref[:, cb * SUBLANES:(cb + 1) * SUBLANES, :].reshape(tile, LANES) for cb in range(CONV_BLOCKS)], axis=1)
    cv = _silu(_layernorm_rows(cv, clng_ref[...], clnb_ref[...]))
    cv = jnp.dot(cv.astype(bf16), wpw_ref[...], preferred_element_type=f32) + bpw_ref[...]
    conv_out = (cv * _silu(proj(G_CONV))).astype(bf16)

    cos = cos_ref[...]
    sin = sin_ref[...]
    lane = lax.broadcasted_iota(jnp.int32, (tile, LANES), 1)
    first_half = (lane % RET_QK_DIM) < (RET_QK_DIM // 2)

    def rotary(t):
        blocks = []
        for p in range(HEAD_PAIRS):
            blk = t[:, p * LANES:(p + 1) * LANES]
            swapped = jnp.where(first_half,
                                pltpu.roll(blk, LANES - RET_QK_DIM // 2, axis=1),
                                pltpu.roll(blk, RET_QK_DIM // 2, axis=1))
            blocks.append(blk * cos + swapped * sin)
        return blocks

    q_blocks = rotary(proj(Q))
    k_blocks = rotary(proj(K))
    v = proj(V).astype(bf16)

    row = lax.broadcasted_iota(jnp.int32, (CHUNK, CHUNK), 0)
    col = lax.broadcasted_iota(jnp.int32, (CHUNK, CHUNK), 1)
    head0_lane = col < RET_QK_DIM
    head0_row = row < RET_QK_DIM
    row2 = lax.broadcasted_iota(jnp.int32, (CHUNK, 2 * CHUNK), 0)
    col2 = lax.broadcasted_iota(jnp.int32, (CHUNK, 2 * CHUNK), 1)
    causal2 = row2 >= (col2 % CHUNK)
    head0_col = col2 < RET_V_DIM
    for c in range(tile // CHUNK):
        rows = slice(c * CHUNK, (c + 1) * CHUNK)
        for p in range(HEAD_PAIRS):
            qb = (q_blocks[p][rows] * qdec_ref[p]).astype(bf16)
            kp = k_blocks[p][rows]
            k_in = kp * kdin_ref[p]
            k_st = (kp * kdst_ref[p]).astype(bf16)
            k_heads = jnp.concatenate([jnp.where(head0_lane, k_in, 0.0),
                                       jnp.where(head0_lane, 0.0, k_in)], axis=0).astype(bf16)
            vp = v[rows, 2 * p * RET_V_DIM:(2 * p + 2) * RET_V_DIM]
            v_diag = jnp.concatenate([jnp.where(head0_col, vp, jnp.zeros_like(vp)),
                                      jnp.where(head0_col, jnp.zeros_like(vp), vp)], axis=0)
            state = state_ref[p]
            st_heads = jnp.concatenate([jnp.where(head0_row, state, 0.0),
                                        jnp.where(head0_row, 0.0, state)], axis=1).astype(bf16)
            s = lax.dot_general(qb, k_heads, (((1,), (1,)), ((), ())), preferred_element_type=f32)
            pm = jnp.where(causal2, s, 0.0).astype(bf16)
            o = (jnp.dot(pm, v_diag, preferred_element_type=f32)
                 + jnp.dot(qb, st_heads, preferred_element_type=f32))
            ret_ref[rows, 2 * p * RET_V_DIM:(2 * p + 2) * RET_V_DIM] = o
            kv = lax.dot_general(k_st, vp, (((0,), (0,)), ((), ())), preferred_element_type=f32)
            state_ref[p] = cdec_ref[p] * state + jnp.where(head0_row, kv[:, :RET_V_DIM], kv[:, RET_V_DIM:])

    g_ret = proj(G_RET)
    ret_parts = []
    for h in range(RET_HEADS):
        cols = slice(h * RET_V_DIM, (h + 1) * RET_V_DIM)
        y = ret_ref[:, cols]
        mu = jnp.mean(y, axis=-1, keepdims=True)
        d = y - mu
        var = jnp.mean(d * d, axis=-1, keepdims=True)
        yn = d * lax.rsqrt(var + LN_EPS) * retg_ref[:, cols]
        ret_parts.append((yn * _silu(g_ret[:, cols])).astype(bf16))
    ret_out = jnp.concatenate(ret_parts, axis=1)

    hmix = (jnp.dot(ret_out, wout_ref[0:RET_WIDTH, :], preferred_element_type=f32)
            + jnp.dot(conv_out, wout_ref[RET_WIDTH:RET_WIDTH + CONV_WIDTH, :], preferred_element_type=f32))
    o_ref[...] = _layernorm_rows(DEEPNORM_ALPHA * x_ref[...] + hmix, plng_ref[...], plnb_ref[...])


def _const_spec(shape):
    zeros = (0,) * len(shape)
    return pl.BlockSpec(shape, lambda b, s: zeros, pipeline_mode=pl.Buffered(1))


def _layer(x, w_in, ret_norm_g, dw_kernel, dw_bias, conv_ln_g, conv_ln_b,
           w_pw2, b_pw2, w_out, post_ln_g, post_ln_b):
    batch, seq, d = x.shape
    tile = SEQ_TILE
    assert d == D_MODEL and seq % tile == 0 and tile % CHUNK == 0
    assert w_in.shape[0] % WIDE_STAGE_ROWS == 0 and w_pw2.shape[1] == w_out.shape[1]
    assert w_pw2.shape[0] % NARROW_STAGE_ROWS == 0 and w_out.shape[0] % NARROW_STAGE_ROWS == 0
    cos, sin = _rope_tables(seq)
    qdec, kdin, kdst, cdec = _decay_tables()
    row2 = lambda a: a.reshape(1, -1)

    args = (x, w_in, w_pw2, w_out, cos, sin,
            qdec, kdin, kdst, cdec,
            row2(ret_norm_g), dw_kernel.reshape(CONV_KERNEL, CONV_BLOCKS, LANES),
            dw_bias.reshape(CONV_BLOCKS, LANES), row2(conv_ln_g), row2(conv_ln_b),
            row2(b_pw2), row2(post_ln_g), row2(post_ln_b))
    in_specs = [
        pl.BlockSpec((None, tile, D_MODEL), lambda b, s: (b, s, 0)),
        pl.BlockSpec(memory_space=pl.ANY), pl.BlockSpec(memory_space=pl.ANY), pl.BlockSpec(memory_space=pl.ANY),
        pl.BlockSpec((tile, LANES), lambda b, s: (s, 0)),
        pl.BlockSpec((tile, LANES), lambda b, s: (s, 0)),
    ] + [_const_spec(a.shape) for a in args[6:]]

    return pl.pallas_call(
        _layer_kernel,
        grid=(batch, seq // tile),
        in_specs=in_specs,
        out_specs=pl.BlockSpec((None, tile, D_MODEL), lambda b, s: (b, s, 0)),
        out_shape=jax.ShapeDtypeStruct(x.shape, x.dtype),
        scratch_shapes=[
            pltpu.VMEM((HEAD_PAIRS, LANES, RET_V_DIM), jnp.float32),
            pltpu.VMEM(((HALO + tile) * SUBLANES, LANES), jnp.float32),
            pltpu.VMEM((tile // SUBLANES, CONV_BLOCKS * SUBLANES, LANES), jnp.float32),
            pltpu.VMEM((tile, RET_WIDTH), jnp.float32),
            pltpu.VMEM(w_in.shape, jnp.bfloat16),
            pltpu.VMEM(w_pw2.shape, jnp.bfloat16),
            pltpu.VMEM(w_out.shape, jnp.bfloat16),
            pltpu.VMEM((STAGE_SLOTS, WIDE_STAGE_ROWS, w_in.shape[1]), jnp.float32),
            pltpu.VMEM((STAGE_SLOTS, NARROW_STAGE_ROWS, w_out.shape[1]), jnp.float32),
            pltpu.SemaphoreType.DMA((STAGE_SLOTS,)),
            pltpu.SemaphoreType.DMA((STAGE_SLOTS,)),
        ],
        compiler_params=pltpu.CompilerParams(
            dimension_semantics=("arbitrary", "arbitrary"),
            vmem_limit_bytes=VMEM_LIMIT_BYTES),
        name="hybrid_layer",
    )(*args)


@jax.jit
def kernel(x, w_in, ret_norm_g, dw_kernel, dw_bias, conv_ln_g, conv_ln_b, w_pw2, b_pw2, w_out, post_ln_g, post_ln_b):
    for layer in range(DEPTH):
        x = _layer(x, w_in[layer], ret_norm_g[layer], dw_kernel[layer], dw_bias[layer],
                   conv_ln_g[layer], conv_ln_b[layer], w_pw2[layer], b_pw2[layer],
                   w_out[layer], post_ln_g[layer], post_ln_b[layer])
    return x
```

```python
import numpy as np
import jax
import jax.numpy as jnp
from jax import lax
from jax.experimental import pallas as pl
from jax.experimental.pallas import tpu as pltpu

D_MODEL = 1024
RET_HEADS = 8
RET_QK_DIM = 64
RET_V_DIM = 128
RET_QK_WIDTH = RET_HEADS * RET_QK_DIM
RET_WIDTH = RET_HEADS * RET_V_DIM
CHUNK = 128
ROPE_THETA = 10000.0
CONV_WIDTH = D_MODEL
CONV_KERNEL = 31
LN_EPS = 1e-5
DEPTH = 1
DEEPNORM_ALPHA = (2.0 * DEPTH) ** 0.25

LANES = 128
SUBLANES = 8
HEAD_PAIRS = RET_HEADS // 2
CONV_BLOCKS = CONV_WIDTH // LANES
HALO = 32
SEQ_TILE = 512
VMEM_LIMIT_BYTES = 56 * 1024 * 1024
STAGE_SLOTS = 4
WIDE_STAGE_ROWS = 64
NARROW_STAGE_ROWS = 256

_OFF = [int(v) for v in np.cumsum([0, RET_QK_WIDTH, RET_QK_WIDTH, RET_WIDTH, RET_WIDTH,
                                   CONV_WIDTH, CONV_WIDTH, CONV_WIDTH])]
Q, K, V, G_RET, A_CONV, GLU_CONV, G_CONV = range(7)


def _decay_tables():
    log_g = np.log1p(-np.exp2(-5.0 - np.arange(RET_HEADS, dtype=np.float64)))
    lane_head = np.arange(LANES) // RET_QK_DIM
    idx = np.arange(CHUNK, dtype=np.float64)
    scale = RET_QK_DIM ** -0.5
    qdec = np.zeros((HEAD_PAIRS, CHUNK, LANES))
    kdec_in = np.zeros((HEAD_PAIRS, CHUNK, LANES))
    kdec_st = np.zeros((HEAD_PAIRS, CHUNK, LANES))
    cdec = np.zeros((HEAD_PAIRS, LANES, RET_V_DIM))
    for p in range(HEAD_PAIRS):
        lg = log_g[2 * p + lane_head]
        qdec[p] = np.exp(lg[None, :] * (idx[:, None] + 1.0))
        kdec_in[p] = scale * np.exp(-lg[None, :] * (idx[:, None] + 1.0))
        kdec_st[p] = scale * np.exp(lg[None, :] * (CHUNK - 1.0 - idx[:, None]))
        cdec[p] = np.exp(lg * CHUNK)[:, None] * np.ones((1, RET_V_DIM))
    f = lambda a: jnp.asarray(a, jnp.float32)
    return f(qdec), f(kdec_in), f(kdec_st), f(cdec)


def _rope_tables(seq):
    half = RET_QK_DIM // 2
    inv_freq = ROPE_THETA ** (-np.arange(0, half, dtype=np.float64) * 2.0 / RET_QK_DIM)
    ang = np.arange(seq, dtype=np.float64)[:, None] * inv_freq[None, :]
    lane = np.arange(LANES)
    cos = np.cos(ang)[:, lane % half]
    sign = np.where((lane % RET_QK_DIM) < half, -1.0, 1.0)
    sin = np.sin(ang)[:, lane % half] * sign[None, :]
    return jnp.asarray(cos, jnp.float32), jnp.asarray(sin, jnp.float32)


def _sigmoid(x):
    return 0.5 * jnp.tanh(0.5 * x) + 0.5


def _silu(x):
    return x * _sigmoid(x)


def _layernorm_rows(x, g, b):
    mu = jnp.mean(x, axis=-1, keepdims=True)
    d = x - mu
    var = jnp.mean(d * d, axis=-1, keepdims=True)
    return d * lax.rsqrt(var + LN_EPS) * g + b


def _load_as_bf16(src_hbm, dst_ref, stage_ref, sems):
    n_slots, chunk = stage_ref.shape[0], stage_ref.shape[1]
    n_chunks = src_hbm.shape[0] // chunk

    def chunk_copy(i):
        slot = i % n_slots
        return pltpu.make_async_copy(src_hbm.at[pl.ds(i * chunk, chunk)], stage_ref.at[slot], sems.at[slot])

    for i in range(min(n_slots - 1, n_chunks)):
        chunk_copy(i).start()
    for i in range(n_chunks):
        ahead = i + n_slots - 1
        if ahead < n_chunks:
            chunk_copy(ahead).start()
        chunk_copy(i).wait()
        dst_ref[pl.ds(i * chunk, chunk), :] = stage_ref[i % n_slots].astype(jnp.bfloat16)


def _layer_kernel(x_ref, win_hbm, wpw_hbm, wout_hbm, cos_ref, sin_ref,
                  qdec_ref, kdin_ref, kdst_ref, cdec_ref,
                  retg_ref, dwk_ref, dwb_ref, clng_ref, clnb_ref, bpw_ref, plng_ref, plnb_ref,
                  o_ref, state_ref, ut_ref, cn_ref, ret_ref,
                  win_ref, wpw_ref, wout_ref, wide_stage, narrow_stage, wide_sems, narrow_sems):
    tile = x_ref.shape[0]
    groups = tile // SUBLANES
    f32, bf16 = jnp.float32, jnp.bfloat16

    @pl.when((pl.program_id(0) == 0) & (pl.program_id(1) == 0))
    def _():
        _load_as_bf16(win_hbm, win_ref, wide_stage, wide_sems)
        _load_as_bf16(wpw_hbm, wpw_ref, narrow_stage, narrow_sems)
        _load_as_bf16(wout_hbm, wout_ref, narrow_stage, narrow_sems)

    @pl.when(pl.program_id(1) == 0)
    def _():
        state_ref[...] = jnp.zeros_like(state_ref)
        ut_ref[0:HALO * SUBLANES, :] = jnp.zeros((HALO * SUBLANES, LANES), f32)

    xb = x_ref[...].astype(bf16)

    def proj(group):
        return jnp.dot(xb, win_ref[:, _OFF[group]:_OFF[group + 1]], preferred_element_type=f32)

    u = proj(A_CONV) * _sigmoid(proj(GLU_CONV))
    for cb in range(CONV_BLOCKS):
        for tg in range(groups):
            ut_ref[pl.ds((HALO + tg * SUBLANES) * SUBLANES + cb, SUBLANES, stride=SUBLANES), :] = (
                u[tg * SUBLANES:(tg + 1) * SUBLANES, cb * LANES:(cb + 1) * LANES])
    taps = [dwk_ref[j] for j in range(CONV_KERNEL)]
    bias = dwb_ref[...]
    span = SUBLANES + CONV_KERNEL - 1
    for tg in range(groups):
        acc = [bias] * SUBLANES
        for m in range(span):
            um = ut_ref[pl.ds((HALO + tg * SUBLANES - (CONV_KERNEL - 1) + m) * SUBLANES, SUBLANES), :]
            for i in range(SUBLANES):
                if 0 <= m - i < CONV_KERNEL:
                    acc[i] = acc[i] + um * taps[m - i]
        for i in range(SUBLANES):
            cn_ref[tg, pl.ds(i, SUBLANES, stride=SUBLANES), :] = acc[i]
    ut_ref[0:HALO * SUBLANES, :] = ut_ref[tile * SUBLANES:(tile + HALO) * SUBLANES, :]
    cv = jnp.concatenate(
        [cn_ref[:, cb * SUBLANES:(cb + 1) * SUBLANES, :].reshape(tile, LANES) for cb in range(CONV_BLOCKS)], axis=1)
    cv = _silu(_layernorm_rows(cv, clng_ref[...], clnb_ref[...]))
    cv = jnp.dot(cv.astype(bf16), wpw_ref[...], preferred_element_type=f32) + bpw_ref[...]
    conv_out = (cv * _silu(proj(G_CONV))).astype(bf16)

    cos = cos_ref[...]
    sin = sin_ref[...]
    lane = lax.broadcasted_iota(jnp.int32, (tile, LANES), 1)
    first_half = (lane % RET_QK_DIM) < (RET_QK_DIM // 2)

    def rotary(t):
        blocks = []
        for p in range(HEAD_PAIRS):
            blk = t[:, p * LANES:(p + 1) * LANES]
            swapped = jnp.where(first_half,
                                pltpu.roll(blk, LANES - RET_QK_DIM // 2, axis=1),
                                pltpu.roll(blk, RET_QK_DIM // 2, axis=1))
            blocks.append(blk * cos + swapped * sin)
        return blocks

    q_blocks = rotary(proj(Q))
    k_blocks = rotary(proj(K))
    v = proj(V).astype(bf16)

    row = lax.broadcasted_iota(jnp.int32, (CHUNK, CHUNK), 0)
    col = lax.broadcasted_iota(jnp.int32, (CHUNK, CHUNK), 1)
    head0_lane = col < RET_QK_DIM
    head0_row = row < RET_QK_DIM
    row2 = lax.broadcasted_iota(jnp.int32, (CHUNK, 2 * CHUNK), 0)
    col2 = lax.broadcasted_iota(jnp.int32, (CHUNK, 2 * CHUNK), 1)
    causal2 = row2 >= (col2 % CHUNK)
    head0_col = col2 < RET_V_DIM
    for c in range(tile // CHUNK):
        rows = slice(c * CHUNK, (c + 1) * CHUNK)
        for p in range(HEAD_PAIRS):
            qb = (q_blocks[p][rows] * qdec_ref[p]).astype(bf16)
            kp = k_blocks[p][rows]
            k_in = kp * kdin_ref[p]
            k_st = (kp * kdst_ref[p]).astype(bf16)
            k_heads = jnp.concatenate([jnp.where(head0_lane, k_in, 0.0),
                                       jnp.where(head0_lane, 0.0, k_in)], axis=0).astype(bf16)
            vp = v[rows, 2 * p * RET_V_DIM:(2 * p + 2) * RET_V_DIM]
            v_diag = jnp.concatenate([jnp.where(head0_col, vp, jnp.zeros_like(vp)),
                                      jnp.where(head0_col, jnp.zeros_like(vp), vp)], axis=0)
            state = state_ref[p]
            st_heads = jnp.concatenate([jnp.where(head0_row, state, 0.0),
                                        jnp.where(head0_row, 0.0, state)], axis=1).astype(bf16)
            s = lax.dot_general(qb, k_heads, (((1,), (1,)), ((), ())), preferred_element_type=f32)
            pm = jnp.where(causal2, s, 0.0).astype(bf16)
            o = (jnp.dot(pm, v_diag, preferred_element_type=f32)
                 + jnp.dot(qb, st_heads, preferred_element_type=f32))
            ret_ref[rows, 2 * p * RET_V_DIM:(2 * p + 2) * RET_V_DIM] = o
            kv = lax.dot_general(k_st, vp, (((0,), (0,)), ((), ())), preferred_element_type=f32)
            state_ref[p] = cdec_ref[p] * state + jnp.where(head0_row, kv[:, :RET_V_DIM], kv[:, RET_V_DIM:])

    g_ret = proj(G_RET)
    ret_parts = []
    for h in range(RET_HEADS):
        cols = slice(h * RET_V_DIM, (h + 1) * RET_V_DIM)
        y = ret_ref[:, cols]
        mu = jnp.mean(y, axis=-1, keepdims=True)
        d = y - mu
        var = jnp.mean(d * d, axis=-1, keepdims=True)
        yn = d * lax.rsqrt(var + LN_EPS) * retg_ref[:, cols]
        ret_parts.append((yn * _silu(g_ret[:, cols])).astype(bf16))
    ret_out = jnp.concatenate(ret_parts, axis=1)

    hmix = (jnp.dot(ret_out, wout_ref[0:RET_WIDTH, :], preferred_element_type=f32)
            + jnp.dot(conv_out, wout_ref[RET_WIDTH:RET_WIDTH + CONV_WIDTH, :], preferred_element_type=f32))
    o_ref[...] = _layernorm_rows(DEEPNORM_ALPHA * x_ref[...] + hmix, plng_ref[...], plnb_ref[...])


def _const_spec(shape):
    zeros = (0,) * len(shape)
    return pl.BlockSpec(shape, lambda b, s: zeros, pipeline_mode=pl.Buffered(1))


def _layer(x, w_in, ret_norm_g, dw_kernel, dw_bias, conv_ln_g, conv_ln_b,
           w_pw2, b_pw2, w_out, post_ln_g, post_ln_b):
    batch, seq, d = x.shape
    tile = SEQ_TILE
    assert d == D_MODEL and seq % tile == 0 and tile % CHUNK == 0
    assert w_in.shape[0] % WIDE_STAGE_ROWS == 0 and w_pw2.shape[1] == w_out.shape[1]
    assert w_pw2.shape[0] % NARROW_STAGE_ROWS == 0 and w_out.shape[0] % NARROW_STAGE_ROWS == 0
    cos, sin = _rope_tables(seq)
    qdec, kdin, kdst, cdec = _decay_tables()
    row2 = lambda a: a.reshape(1, -1)

    args = (x, w_in, w_pw2, w_out, cos, sin,
            qdec, kdin, kdst, cdec,
            row2(ret_norm_g), dw_kernel.reshape(CONV_KERNEL, CONV_BLOCKS, LANES),
            dw_bias.reshape(CONV_BLOCKS, LANES), row2(conv_ln_g), row2(conv_ln_b),
            row2(b_pw2), row2(post_ln_g), row2(post_ln_b))
    in_specs = [
        pl.BlockSpec((None, tile, D_MODEL), lambda b, s: (b, s, 0)),
        pl.BlockSpec(memory_space=pl.ANY), pl.BlockSpec(memory_space=pl.ANY), pl.BlockSpec(memory_space=pl.ANY),
        pl.BlockSpec((tile, LANES), lambda b, s: (s, 0)),
        pl.BlockSpec((tile, LANES), lambda b, s: (s, 0)),
    ] + [_const_spec(a.shape) for a in args[6:]]

    return pl.pallas_call(
        _layer_kernel,
        grid=(batch, seq // tile),
        in_specs=in_specs,
        out_specs=pl.BlockSpec((None, tile, D_MODEL), lambda b, s: (b, s, 0)),
        out_shape=jax.ShapeDtypeStruct(x.shape, x.dtype),
        scratch_shapes=[
            pltpu.VMEM((HEAD_PAIRS, LANES, RET_V_DIM), jnp.float32),
            pltpu.VMEM(((HALO + tile) * SUBLANES, LANES), jnp.float32),
            pltpu.VMEM((tile // SUBLANES, CONV_BLOCKS * SUBLANES, LANES), jnp.float32),
            pltpu.VMEM((tile, RET_WIDTH), jnp.float32),
            pltpu.VMEM(w_in.shape, jnp.bfloat16),
            pltpu.VMEM(w_pw2.shape, jnp.bfloat16),
            pltpu.VMEM(w_out.shape, jnp.bfloat16),
            pltpu.VMEM((STAGE_SLOTS, WIDE_STAGE_ROWS, w_in.shape[1]), jnp.float32),
            pltpu.VMEM((STAGE_SLOTS, NARROW_STAGE_ROWS, w_out.shape[1]), jnp.float32),
            pltpu.SemaphoreType.DMA((STAGE_SLOTS,)),
            pltpu.SemaphoreType.DMA((STAGE_SLOTS,)),
        ],
        compiler_params=pltpu.CompilerParams(
            dimension_semantics=("arbitrary", "arbitrary"),
            vmem_limit_bytes=VMEM_LIMIT_BYTES),
        name="hybrid_layer",
    )(*args)


@jax.jit
def kernel(x, w_in, ret_norm_g, dw_kernel, dw_bias, conv_ln_g, conv_ln_b, w_pw2, b_pw2, w_out, post_ln_g, post_ln_b):
    for layer in range(DEPTH):
        x = _layer(x, w_in[layer], ret_norm_g[layer], dw_kernel[layer], dw_bias[layer],
                   conv_ln_g[layer], conv_ln_b[layer], w_pw2[layer], b_pw2[layer],
                   w_out[layer], post_ln_g[layer], post_ln_b[layer])
    return x
```

```python
import numpy as np
import jax
import jax.numpy as jnp
from jax import lax
from jax.experimental import pallas as pl
from jax.experimental.pallas import tpu as pltpu

D_MODEL = 1024
RET_HEADS = 8
RET_QK_DIM = 64
RET_V_DIM = 128
RET_QK_WIDTH = RET_HEADS * RET_QK_DIM
RET_WIDTH = RET_HEADS * RET_V_DIM
CHUNK = 128
ROPE_THETA = 10000.0
CONV_WIDTH = D_MODEL
CONV_KERNEL = 31
LN_EPS = 1e-5
DEPTH = 1
DEEPNORM_ALPHA = (2.0 * DEPTH) ** 0.25

LANES = 128
SUBLANES = 8
HEAD_PAIRS = RET_HEADS // 2
CONV_BLOCKS = CONV_WIDTH // LANES
FIR_TAPS = 32
FIR_SPLITS = 4
HALO = FIR_TAPS
SEQ_TILE = 512
VMEM_LIMIT_BYTES = 56 * 1024 * 1024
STAGE_SLOTS = 4
WIDE_STAGE_ROWS = 64
NARROW_STAGE_ROWS = 256

_OFF = [int(v) for v in np.cumsum([0, RET_QK_WIDTH, RET_QK_WIDTH, RET_WIDTH, RET_WIDTH,
                                   CONV_WIDTH, CONV_WIDTH, CONV_WIDTH])]
Q, K, V, G_RET, A_CONV, GLU_CONV, G_CONV = range(7)


def _decay_tables():
    log_g = np.log1p(-np.exp2(-5.0 - np.arange(RET_HEADS, dtype=np.float64)))
    lane_head = np.arange(LANES) // RET_QK_DIM
    idx = np.arange(CHUNK, dtype=np.float64)
    scale = RET_QK_DIM ** -0.5
    qdec = np.zeros((HEAD_PAIRS, CHUNK, LANES))
    kdec_in = np.zeros((HEAD_PAIRS, CHUNK, LANES))
    kdec_st = np.zeros((HEAD_PAIRS, CHUNK, LANES))
    cdec = np.zeros((HEAD_PAIRS, LANES, RET_V_DIM))
    for p in range(HEAD_PAIRS):
        lg = log_g[2 * p + lane_head]
        qdec[p] = np.exp(lg[None, :] * (idx[:, None] + 1.0))
        kdec_in[p] = scale * np.exp(-lg[None, :] * (idx[:, None] + 1.0))
        kdec_st[p] = scale * np.exp(lg[None, :] * (CHUNK - 1.0 - idx[:, None]))
        cdec[p] = np.exp(lg * CHUNK)[:, None] * np.ones((1, RET_V_DIM))
    f = lambda a: jnp.asarray(a, jnp.float32)
    return f(qdec), f(kdec_in), f(kdec_st), f(cdec)


def _rope_tables(seq):
    half = RET_QK_DIM // 2
    inv_freq = ROPE_THETA ** (-np.arange(0, half, dtype=np.float64) * 2.0 / RET_QK_DIM)
    ang = np.arange(seq, dtype=np.float64)[:, None] * inv_freq[None, :]
    lane = np.arange(LANES)
    cos = np.cos(ang)[:, lane % half]
    sign = np.where((lane % RET_QK_DIM) < half, -1.0, 1.0)
    sin = np.sin(ang)[:, lane % half] * sign[None, :]
    return jnp.asarray(cos, jnp.float32), jnp.asarray(sin, jnp.float32)


def _sigmoid(x):
    return 0.5 * jnp.tanh(0.5 * x) + 0.5


def _silu(x):
    return x * _sigmoid(x)


def _layernorm_rows(x, g, b):
    mu = jnp.mean(x, axis=-1, keepdims=True)
    d = x - mu
    var = jnp.mean(d * d, axis=-1, keepdims=True)
    return d * lax.rsqrt(var + LN_EPS) * g + b


def _memo(fn):
    cache = {}

    def get(i):
        if i not in cache:
            cache[i] = fn(i)
        return cache[i]

    return get


def _tap_sum(a, b):
    if a is None:
        return b
    if b is None:
        return a
    return a + b


def _fast_fir(z, taps, splits):
    n_taps = len(taps)
    if splits == 0:
        assert any(tap is not None for tap in taps)

        def direct(k):
            acc = None
            for p, tap in enumerate(taps):
                if tap is not None:
                    term = z(k + n_taps - 1 - p) * tap
                    acc = term if acc is None else acc + term
            return acc
        return _memo(direct)

    even, odd = taps[0::2], taps[1::2]
    a = lambda m: z(2 * m + 1)
    b = lambda m: z(2 * m + 2)
    fa = _fast_fir(a, even, splits - 1)
    fb = _fast_fir(b, odd, splits - 1)
    fc = _fast_fir(_memo(lambda m: a(m) + b(m)), [_tap_sum(e, o) for e, o in zip(even, odd)], splits - 1)

    def combined(k):
        q = k // 2
        if k % 2 == 0:
            return fa(q) + fb(q - 1)
        return fc(q) - fa(q) - fb(q)

    return _memo(combined)


def _load_as_bf16(src_hbm, dst_ref, stage_ref, sems):
    n_slots, chunk = stage_ref.shape[0], stage_ref.shape[1]
    n_chunks = src_hbm.shape[0] // chunk

    def chunk_copy(i):
        slot = i % n_slots
        return pltpu.make_async_copy(src_hbm.at[pl.ds(i * chunk, chunk)], stage_ref.at[slot], sems.at[slot])

    for i in range(min(n_slots - 1, n_chunks)):
        chunk_copy(i).start()
    for i in range(n_chunks):
        ahead = i + n_slots - 1
        if ahead < n_chunks:
            chunk_copy(ahead).start()
        chunk_copy(i).wait()
        dst_ref[pl.ds(i * chunk, chunk), :] = stage_ref[i % n_slots].astype(jnp.bfloat16)


def _layer_kernel(x_ref, win_hbm, wpw_hbm, wout_hbm, cos_ref, sin_ref,
                  qdec_ref, kdin_ref, kdst_ref, cdec_ref,
                  retg_ref, dwk_ref, dwb_ref, clng_ref, clnb_ref, bpw_ref, plng_ref, plnb_ref,
                  o_ref, state_ref, ut_ref, cn_ref, ret_ref,
                  win_ref, wpw_ref, wout_ref, wide_stage, narrow_stage, wide_sems, narrow_sems):
    tile = x_ref.shape[0]
    groups = tile // SUBLANES
    f32, bf16 = jnp.float32, jnp.bfloat16

    @pl.when((pl.program_id(0) == 0) & (pl.program_id(1) == 0))
    def _():
        _load_as_bf16(win_hbm, win_ref, wide_stage, wide_sems)
        _load_as_bf16(wpw_hbm, wpw_ref, narrow_stage, narrow_sems)
        _load_as_bf16(wout_hbm, wout_ref, narrow_stage, narrow_sems)

    @pl.when(pl.program_id(1) == 0)
    def _():
        state_ref[...] = jnp.zeros_like(state_ref)
        ut_ref[0:HALO * SUBLANES, :] = jnp.zeros((HALO * SUBLANES, LANES), f32)

    xb = x_ref[...].astype(bf16)

    def proj(group):
        return jnp.dot(xb, win_ref[:, _OFF[group]:_OFF[group + 1]], preferred_element_type=f32)

    u = proj(A_CONV) * _sigmoid(proj(GLU_CONV))
    for cb in range(CONV_BLOCKS):
        for tg in range(groups):
            ut_ref[pl.ds((HALO + tg * SUBLANES) * SUBLANES + cb, SUBLANES, stride=SUBLANES), :] = (
                u[tg * SUBLANES:(tg + 1) * SUBLANES, cb * LANES:(cb + 1) * LANES])
    taps = [dwk_ref[CONV_KERNEL - 1 - j] for j in range(CONV_KERNEL)] + [None] * (FIR_TAPS - CONV_KERNEL)
    conv_in = _memo(lambda i: ut_ref[pl.ds((i - (FIR_TAPS - 1) + HALO) * SUBLANES, SUBLANES), :])
    conv_out_t = _fast_fir(conv_in, taps, FIR_SPLITS)
    bias = dwb_ref[...]
    for t in range(tile):
        cn_ref[t // SUBLANES, pl.ds(t % SUBLANES, SUBLANES, stride=SUBLANES), :] = conv_out_t(t) + bias
    ut_ref[0:HALO * SUBLANES, :] = ut_ref[tile * SUBLANES:(tile + HALO) * SUBLANES, :]
    cv = jnp.concatenate(
        [cn_ref[:, cb * SUBLANES:(cb + 1) * SUBLANES, :].reshape(tile, LANES) for cb in range(CONV_BLOCKS)], axis=1)
    cv = _silu(_layernorm_rows(cv, clng_ref[...], clnb_ref[...]))
    cv = jnp.dot(cv.astype(bf16), wpw_ref[...], preferred_element_type=f32) + bpw_ref[...]
    conv_out = (cv * _silu(proj(G_CONV))).astype(bf16)

    cos = cos_ref[...]
    sin = sin_ref[...]
    lane = lax.broadcasted_iota(jnp.int32, (tile, LANES), 1)
    first_half = (lane % RET_QK_DIM) < (RET_QK_DIM // 2)

    def rotary(t):
        blocks = []
        for p in range(HEAD_PAIRS):
            blk = t[:, p * LANES:(p + 1) * LANES]
            swapped = jnp.where(first_half,
                                pltpu.roll(blk, LANES - RET_QK_DIM // 2, axis=1),
                                pltpu.roll(blk, RET_QK_DIM // 2, axis=1))
            blocks.append(blk * cos + swapped * sin)
        return blocks

    q_blocks = rotary(proj(Q))
    k_blocks = rotary(proj(K))
    v = proj(V).astype(bf16)

    row = lax.broadcasted_iota(jnp.int32, (CHUNK, CHUNK), 0)
    col = lax.broadcasted_iota(jnp.int32, (CHUNK, CHUNK), 1)
    head0_lane = col < RET_QK_DIM
    head0_row = row < RET_QK_DIM
    row2 = lax.broadcasted_iota(jnp.int32, (CHUNK, 2 * CHUNK), 0)
    col2 = lax.broadcasted_iota(jnp.int32, (CHUNK, 2 * CHUNK), 1)
    causal2 = row2 >= (col2 % CHUNK)
    head0_col = col2 < RET_V_DIM
    for c in range(tile // CHUNK):
        rows = slice(c * CHUNK, (c + 1) * CHUNK)
        for p in range(HEAD_PAIRS):
            qb = (q_blocks[p][rows] * qdec_ref[p]).astype(bf16)
            kp = k_blocks[p][rows]
            k_in = kp * kdin_ref[p]
            k_st = (kp * kdst_ref[p]).astype(bf16)
            k_heads = jnp.concatenate([jnp.where(head0_lane, k_in, 0.0),
                                       jnp.where(head0_lane, 0.0, k_in)], axis=0).astype(bf16)
            vp = v[rows, 2 * p * RET_V_DIM:(2 * p + 2) * RET_V_DIM]
            v_diag = jnp.concatenate([jnp.where(head0_col, vp, jnp.zeros_like(vp)),
                                      jnp.where(head0_col, jnp.zeros_like(vp), vp)], axis=0)
            state = state_ref[p]
            st_heads = jnp.concatenate([jnp.where(head0_row, state, 0.0),
                                        jnp.where(head0_row, 0.0, state)], axis=1).astype(bf16)
            s = lax.dot_general(qb, k_heads, (((1,), (1,)), ((), ())), preferred_element_type=f32)
            pm = jnp.where(causal2, s, 0.0).astype(bf16)
            o = (jnp.dot(pm, v_diag, preferred_element_type=f32)
                 + jnp.dot(qb, st_heads, preferred_element_type=f32))
            ret_ref[rows, 2 * p * RET_V_DIM:(2 * p + 2) * RET_V_DIM] = o
            kv = lax.dot_general(k_st, vp, (((0,), (0,)), ((), ())), preferred_element_type=f32)
            state_ref[p] = cdec_ref[p] * state + jnp.where(head0_row, kv[:, :RET_V_DIM], kv[:, RET_V_DIM:])

    g_ret = proj(G_RET)
    ret_parts = []
    for h in range(RET_HEADS):
        cols = slice(h * RET_V_DIM, (h + 1) * RET_V_DIM)
        y = ret_ref[:, cols]
        mu = jnp.mean(y, axis=-1, keepdims=True)
        d = y - mu
        var = jnp.mean(d * d, axis=-1, keepdims=True)
        yn = d * lax.rsqrt(var + LN_EPS) * retg_ref[:, cols]
        ret_parts.append((yn * _silu(g_ret[:, cols])).astype(bf16))
    ret_out = jnp.concatenate(ret_parts, axis=1)

    hmix = (jnp.dot(ret_out, wout_ref[0:RET_WIDTH, :], preferred_element_type=f32)
            + jnp.dot(conv_out, wout_ref[RET_WIDTH:RET_WIDTH + CONV_WIDTH, :], preferred_element_type=f32))
    o_ref[...] = _layernorm_rows(DEEPNORM_ALPHA * x_ref[...] + hmix, plng_ref[...], plnb_ref[...])


def _const_spec(shape):
    zeros = (0,) * len(shape)
    return pl.BlockSpec(shape, lambda b, s: zeros, pipeline_mode=pl.Buffered(1))


def _layer(x, w_in, ret_norm_g, dw_kernel, dw_bias, conv_ln_g, conv_ln_b,
           w_pw2, b_pw2, w_out, post_ln_g, post_ln_b):
    batch, seq, d = x.shape
    tile = SEQ_TILE
    assert d == D_MODEL and seq % tile == 0 and tile % CHUNK == 0
    assert CONV_KERNEL <= FIR_TAPS == 2 ** FIR_SPLITS * (FIR_TAPS >> FIR_SPLITS) and tile % 2 ** FIR_SPLITS == 0
    assert w_in.shape[0] % WIDE_STAGE_ROWS == 0 and w_pw2.shape[1] == w_out.shape[1]
    assert w_pw2.shape[0] % NARROW_STAGE_ROWS == 0 and w_out.shape[0] % NARROW_STAGE_ROWS == 0
    cos, sin = _rope_tables(seq)
    qdec, kdin, kdst, cdec = _decay_tables()
    row2 = lambda a: a.reshape(1, -1)

    args = (x, w_in, w_pw2, w_out, cos, sin,
            qdec, kdin, kdst, cdec,
            row2(ret_norm_g), dw_kernel.reshape(CONV_KERNEL, CONV_BLOCKS, LANES),
            dw_bias.reshape(CONV_BLOCKS, LANES), row2(conv_ln_g), row2(conv_ln_b),
            row2(b_pw2), row2(post_ln_g), row2(post_ln_b))
    in_specs = [
        pl.BlockSpec((None, tile, D_MODEL), lambda b, s: (b, s, 0)),
        pl.BlockSpec(memory_space=pl.ANY), pl.BlockSpec(memory_space=pl.ANY), pl.BlockSpec(memory_space=pl.ANY),
        pl.BlockSpec((tile, LANES), lambda b, s: (s, 0)),
        pl.BlockSpec((tile, LANES), lambda b, s: (s, 0)),
    ] + [_const_spec(a.shape) for a in args[6:]]

    return pl.pallas_call(
        _layer_kernel,
        grid=(batch, seq // tile),
        in_specs=in_specs,
        out_specs=pl.BlockSpec((None, tile, D_MODEL), lambda b, s: (b, s, 0)),
        out_shape=jax.ShapeDtypeStruct(x.shape, x.dtype),
        scratch_shapes=[
            pltpu.VMEM((HEAD_PAIRS, LANES, RET_V_DIM), jnp.float32),
            pltpu.VMEM(((HALO + tile) * SUBLANES, LANES), jnp.float32),
            pltpu.VMEM((tile // SUBLANES, CONV_BLOCKS * SUBLANES, LANES), jnp.float32),
            pltpu.VMEM((tile, RET_WIDTH), jnp.float32),
            pltpu.VMEM(w_in.shape, jnp.bfloat16),
            pltpu.VMEM(w_pw2.shape, jnp.bfloat16),
            pltpu.VMEM(w_out.shape, jnp.bfloat16),
            pltpu.VMEM((STAGE_SLOTS, WIDE_STAGE_ROWS, w_in.shape[1]), jnp.float32),
            pltpu.VMEM((STAGE_SLOTS, NARROW_STAGE_ROWS, w_out.shape[1]), jnp.float32),
            pltpu.SemaphoreType.DMA((STAGE_SLOTS,)),
            pltpu.SemaphoreType.DMA((STAGE_SLOTS,)),
        ],
        compiler_params=pltpu.CompilerParams(
            dimension_semantics=("arbitrary", "arbitrary"),
            vmem_limit_bytes=VMEM_LIMIT_BYTES),
        name="hybrid_layer",
    )(*args)


@jax.jit
def kernel(x, w_in, ret_norm_g, dw_kernel, dw_bias, conv_ln_g, conv_ln_b, w_pw2, b_pw2, w_out, post_ln_g, post_ln_b):
    for layer in range(DEPTH):
        x = _layer(x, w_in[layer], ret_norm_g[layer], dw_kernel[layer], dw_bias[layer],
                   conv_ln_g[layer], conv_ln_b[layer], w_pw2[layer], b_pw2[layer],
                   w_out[layer], post_ln_g[layer], post_ln_b[layer])
    return x
```

```python
import numpy as np
import jax
import jax.numpy as jnp
from jax import lax
from jax.experimental import pallas as pl
from jax.experimental.pallas import tpu as pltpu

D_MODEL = 1024
RET_HEADS = 8
RET_QK_DIM = 64
RET_V_DIM = 128
RET_QK_WIDTH = RET_HEADS * RET_QK_DIM
RET_WIDTH = RET_HEADS * RET_V_DIM
CHUNK = 128
ROPE_THETA = 10000.0
CONV_WIDTH = D_MODEL
CONV_KERNEL = 31
LN_EPS = 1e-5
DEPTH = 1
DEEPNORM_ALPHA = (2.0 * DEPTH) ** 0.25

LANES = 128
SUBLANES = 8
HEAD_PAIRS = RET_HEADS // 2
CONV_BLOCKS = CONV_WIDTH // LANES
FIR_TAPS = 32
FIR_SPLITS = 4
HALO = FIR_TAPS
SEQ_TILE = 512
GLU_PARTS = 4
CONV_TAIL_PARTS = 2
TAIL_PARTS = 2
VMEM_LIMIT_BYTES = 56 * 1024 * 1024
STAGE_SLOTS = 4
WIDE_STAGE_ROWS = 64
NARROW_STAGE_ROWS = 256

_OFF = [int(v) for v in np.cumsum([0, RET_QK_WIDTH, RET_QK_WIDTH, RET_WIDTH, RET_WIDTH,
                                   CONV_WIDTH, CONV_WIDTH, CONV_WIDTH])]
Q, K, V, G_RET, A_CONV, GLU_CONV, G_CONV = range(7)


def _decay_tables():
    log_g = np.log1p(-np.exp2(-5.0 - np.arange(RET_HEADS, dtype=np.float64)))
    lane_head = np.arange(LANES) // RET_QK_DIM
    idx = np.arange(CHUNK, dtype=np.float64)
    scale = RET_QK_DIM ** -0.5
    qdec = np.zeros((HEAD_PAIRS, CHUNK, LANES))
    kdec_in = np.zeros((HEAD_PAIRS, CHUNK, LANES))
    kdec_st = np.zeros((HEAD_PAIRS, CHUNK, LANES))
    cdec = np.zeros((HEAD_PAIRS, LANES, RET_V_DIM))
    for p in range(HEAD_PAIRS):
        lg = log_g[2 * p + lane_head]
        qdec[p] = np.exp(lg[None, :] * (idx[:, None] + 1.0))
        kdec_in[p] = scale * np.exp(-lg[None, :] * (idx[:, None] + 1.0))
        kdec_st[p] = scale * np.exp(lg[None, :] * (CHUNK - 1.0 - idx[:, None]))
        cdec[p] = np.exp(lg * CHUNK)[:, None] * np.ones((1, RET_V_DIM))
    f = lambda a: jnp.asarray(a, jnp.float32)
    return f(qdec), f(kdec_in), f(kdec_st), f(cdec)


def _rope_tables(seq):
    half = RET_QK_DIM // 2
    inv_freq = ROPE_THETA ** (-np.arange(0, half, dtype=np.float64) * 2.0 / RET_QK_DIM)
    ang = np.arange(seq, dtype=np.float64)[:, None] * inv_freq[None, :]
    lane = np.arange(LANES)
    cos = np.cos(ang)[:, lane % half]
    sign = np.where((lane % RET_QK_DIM) < half, -1.0, 1.0)
    sin = np.sin(ang)[:, lane % half] * sign[None, :]
    return jnp.asarray(cos, jnp.float32), jnp.asarray(sin, jnp.float32)


def _sigmoid(x):
    return 0.5 * jnp.tanh(0.5 * x) + 0.5


def _silu(x):
    return x * _sigmoid(x)


def _layernorm_rows(x, g, b):
    mu = jnp.mean(x, axis=-1, keepdims=True)
    d = x - mu
    var = jnp.mean(d * d, axis=-1, keepdims=True)
    return d * lax.rsqrt(var + LN_EPS) * g + b


def _memo(fn):
    cache = {}

    def get(i):
        if i not in cache:
            cache[i] = fn(i)
        return cache[i]

    return get


def _tap_sum(a, b):
    if a is None:
        return b
    if b is None:
        return a
    return a + b


def _fast_fir(z, taps, splits):
    n_taps = len(taps)
    if splits == 0:
        assert any(tap is not None for tap in taps)

        def direct(k):
            acc = None
            for p, tap in enumerate(taps):
                if tap is not None:
                    term = z(k + n_taps - 1 - p) * tap
                    acc = term if acc is None else acc + term
            return acc
        return _memo(direct)

    even, odd = taps[0::2], taps[1::2]
    a = lambda m: z(2 * m + 1)
    b = lambda m: z(2 * m + 2)
    fa = _fast_fir(a, even, splits - 1)
    fb = _fast_fir(b, odd, splits - 1)
    fc = _fast_fir(_memo(lambda m: a(m) + b(m)), [_tap_sum(e, o) for e, o in zip(even, odd)], splits - 1)

    def combined(k):
        q = k // 2
        if k % 2 == 0:
            return fa(q) + fb(q - 1)
        return fc(q) - fa(q) - fb(q)

    return _memo(combined)


def _load_as_bf16(src_hbm, dst_ref, stage_ref, sems):
    n_slots, chunk = stage_ref.shape[0], stage_ref.shape[1]
    n_chunks = src_hbm.shape[0] // chunk

    def chunk_copy(i):
        slot = i % n_slots
        return pltpu.make_async_copy(src_hbm.at[pl.ds(i * chunk, chunk)], stage_ref.at[slot], sems.at[slot])

    for i in range(min(n_slots - 1, n_chunks)):
        chunk_copy(i).start()
    for i in range(n_chunks):
        ahead = i + n_slots - 1
        if ahead < n_chunks:
            chunk_copy(ahead).start()
        chunk_copy(i).wait()
        dst_ref[pl.ds(i * chunk, chunk), :] = stage_ref[i % n_slots].astype(jnp.bfloat16)


def _layer_kernel(x_ref, win_hbm, wpw_hbm, wout_hbm, cos_ref, sin_ref,
                  qdec_ref, kdin_ref, kdst_ref, cdec_ref,
                  retg_ref, dwk_ref, dwb_ref, clng_ref, clnb_ref, bpw_ref, plng_ref, plnb_ref,
                  o_ref, state_ref, ut_ref, cn_ref, ret_ref,
                  win_ref, wpw_ref, wout_ref, wide_stage, narrow_stage, wide_sems, narrow_sems):
    tile = x_ref.shape[0]
    groups = tile // SUBLANES
    f32, bf16 = jnp.float32, jnp.bfloat16

    @pl.when((pl.program_id(0) == 0) & (pl.program_id(1) == 0))
    def _():
        _load_as_bf16(win_hbm, win_ref, wide_stage, wide_sems)
        _load_as_bf16(wpw_hbm, wpw_ref, narrow_stage, narrow_sems)
        _load_as_bf16(wout_hbm, wout_ref, narrow_stage, narrow_sems)

    @pl.when(pl.program_id(1) == 0)
    def _():
        state_ref[...] = jnp.zeros_like(state_ref)
        ut_ref[0:HALO * SUBLANES, :] = jnp.zeros((HALO * SUBLANES, LANES), f32)

    xb = x_ref[...].astype(bf16)

    def proj(group):
        return jnp.dot(xb, win_ref[:, _OFF[group]:_OFF[group + 1]], preferred_element_type=f32)


    for part in range(GLU_PARTS):
        c0 = part * CONV_WIDTH // GLU_PARTS
        c1 = (part + 1) * CONV_WIDTH // GLU_PARTS
        a_part = jnp.dot(xb, win_ref[:, _OFF[A_CONV] + c0:_OFF[A_CONV] + c1], preferred_element_type=f32)
        gate = jnp.dot(xb, win_ref[:, _OFF[GLU_CONV] + c0:_OFF[GLU_CONV] + c1], preferred_element_type=f32)
        u = a_part * _sigmoid(gate)
        for cb in range(c0 // LANES, c1 // LANES):
            for tg in range(groups):
                ut_ref[pl.ds((HALO + tg * SUBLANES) * SUBLANES + cb, SUBLANES, stride=SUBLANES), :] = (
                    u[tg * SUBLANES:(tg + 1) * SUBLANES, cb * LANES - c0:(cb + 1) * LANES - c0])

    cos = cos_ref[...]
    sin = sin_ref[...]
    lane = lax.broadcasted_iota(jnp.int32, (tile, LANES), 1)
    first_half = (lane % RET_QK_DIM) < (RET_QK_DIM // 2)

    def rotary(t):
        blocks = []
        for p in range(HEAD_PAIRS):
            blk = t[:, p * LANES:(p + 1) * LANES]
            swapped = jnp.where(first_half,
                                pltpu.roll(blk, LANES - RET_QK_DIM // 2, axis=1),
                                pltpu.roll(blk, RET_QK_DIM // 2, axis=1))
            blocks.append(blk * cos + swapped * sin)
        return blocks

    q_blocks = rotary(proj(Q))
    k_blocks = rotary(proj(K))
    v = proj(V).astype(bf16)

    row = lax.broadcasted_iota(jnp.int32, (CHUNK, CHUNK), 0)
    col = lax.broadcasted_iota(jnp.int32, (CHUNK, CHUNK), 1)
    head0_lane = col < RET_QK_DIM
    head0_row = row < RET_QK_DIM
    row2 = lax.broadcasted_iota(jnp.int32, (CHUNK, 2 * CHUNK), 0)
    col2 = lax.broadcasted_iota(jnp.int32, (CHUNK, 2 * CHUNK), 1)
    causal2 = row2 >= (col2 % CHUNK)
    head0_col = col2 < RET_V_DIM
    for c in range(tile // CHUNK):
        rows = slice(c * CHUNK, (c + 1) * CHUNK)
        for p in range(HEAD_PAIRS):
            qb = (q_blocks[p][rows] * qdec_ref[p]).astype(bf16)
            kp = k_blocks[p][rows]
            k_in = kp * kdin_ref[p]
            k_st = (kp * kdst_ref[p]).astype(bf16)
            k_heads = jnp.concatenate([jnp.where(head0_lane, k_in, 0.0),
                                       jnp.where(head0_lane, 0.0, k_in)], axis=0).astype(bf16)
            vp = v[rows, 2 * p * RET_V_DIM:(2 * p + 2) * RET_V_DIM]
            v_diag = jnp.concatenate([jnp.where(head0_col, vp, jnp.zeros_like(vp)),
                                      jnp.where(head0_col, jnp.zeros_like(vp), vp)], axis=0)
            state = state_ref[p]
            st_heads = jnp.concatenate([jnp.where(head0_row, state, 0.0),
                                        jnp.where(head0_row, 0.0, state)], axis=1).astype(bf16)
            s = lax.dot_general(qb, k_heads, (((1,), (1,)), ((), ())), preferred_element_type=f32)
            pm = jnp.where(causal2, s, 0.0).astype(bf16)
            o = (jnp.dot(pm, v_diag, preferred_element_type=f32)
                 + jnp.dot(qb, st_heads, preferred_element_type=f32))
            ret_ref[rows, 2 * p * RET_V_DIM:(2 * p + 2) * RET_V_DIM] = o
            kv = lax.dot_general(k_st, vp, (((0,), (0,)), ((), ())), preferred_element_type=f32)
            state_ref[p] = cdec_ref[p] * state + jnp.where(head0_row, kv[:, :RET_V_DIM], kv[:, RET_V_DIM:])

    taps = [dwk_ref[CONV_KERNEL - 1 - j] for j in range(CONV_KERNEL)] + [None] * (FIR_TAPS - CONV_KERNEL)
    conv_in = _memo(lambda i: ut_ref[pl.ds((i - (FIR_TAPS - 1) + HALO) * SUBLANES, SUBLANES), :])
    conv_out_t = _fast_fir(conv_in, taps, FIR_SPLITS)
    bias = dwb_ref[...]
    for t in range(tile):
        cn_ref[t // SUBLANES, pl.ds(t % SUBLANES, SUBLANES, stride=SUBLANES), :] = conv_out_t(t) + bias
    ut_ref[0:HALO * SUBLANES, :] = ut_ref[tile * SUBLANES:(tile + HALO) * SUBLANES, :]

    g_ret = proj(G_RET)
    ret_parts = []
    for h in range(RET_HEADS):
        cols = slice(h * RET_V_DIM, (h + 1) * RET_V_DIM)
        y = ret_ref[:, cols]
        mu = jnp.mean(y, axis=-1, keepdims=True)
        d = y - mu
        var = jnp.mean(d * d, axis=-1, keepdims=True)
        yn = d * lax.rsqrt(var + LN_EPS) * retg_ref[:, cols]
        ret_parts.append((yn * _silu(g_ret[:, cols])).astype(bf16))
    ret_out = jnp.concatenate(ret_parts, axis=1)

    g_conv = proj(G_CONV)
    conv_parts = []
    for part in range(CONV_TAIL_PARTS):
        g0, g1 = part * groups // CONV_TAIL_PARTS, (part + 1) * groups // CONV_TAIL_PARTS
        rows = slice(g0 * SUBLANES, g1 * SUBLANES)
        cv = jnp.concatenate(
            [cn_ref[g0:g1, cb * SUBLANES:(cb + 1) * SUBLANES, :].reshape((g1 - g0) * SUBLANES, LANES)
             for cb in range(CONV_BLOCKS)], axis=1)
        cv = _silu(_layernorm_rows(cv, clng_ref[...], clnb_ref[...]))
        cv = jnp.dot(cv.astype(bf16), wpw_ref[...], preferred_element_type=f32) + bpw_ref[...]
        conv_parts.append((cv * _silu(g_conv[rows])).astype(bf16))
    conv_out = jnp.concatenate(conv_parts, axis=0)

    for part in range(TAIL_PARTS):
        rows = slice(part * tile // TAIL_PARTS, (part + 1) * tile // TAIL_PARTS)
        hmix = (jnp.dot(ret_out[rows], wout_ref[0:RET_WIDTH, :], preferred_element_type=f32)
                + jnp.dot(conv_out[rows], wout_ref[RET_WIDTH:RET_WIDTH + CONV_WIDTH, :], preferred_element_type=f32))
        o_ref[rows, :] = _layernorm_rows(DEEPNORM_ALPHA * x_ref[rows, :] + hmix, plng_ref[...], plnb_ref[...])


def _const_spec(shape):
    zeros = (0,) * len(shape)
    return pl.BlockSpec(shape, lambda b, s: zeros, pipeline_mode=pl.Buffered(1))


def _layer(x, w_in, ret_norm_g, dw_kernel, dw_bias, conv_ln_g, conv_ln_b,
           w_pw2, b_pw2, w_out, post_ln_g, post_ln_b):
    batch, seq, d = x.shape
    tile = SEQ_TILE
    assert d == D_MODEL and seq % tile == 0 and tile % CHUNK == 0
    assert CONV_KERNEL <= FIR_TAPS == 2 ** FIR_SPLITS * (FIR_TAPS >> FIR_SPLITS) and tile % 2 ** FIR_SPLITS == 0
    assert CONV_WIDTH % (GLU_PARTS * LANES) == 0 and tile % (SUBLANES * CONV_TAIL_PARTS) == 0 and tile % TAIL_PARTS == 0
    assert w_in.shape[0] % WIDE_STAGE_ROWS == 0 and w_pw2.shape[1] == w_out.shape[1]
    assert w_pw2.shape[0] % NARROW_STAGE_ROWS == 0 and w_out.shape[0] % NARROW_STAGE_ROWS == 0
    cos, sin = _rope_tables(seq)
    qdec, kdin, kdst, cdec = _decay_tables()
    row2 = lambda a: a.reshape(1, -1)

    args = (x, w_in, w_pw2, w_out, cos, sin,
            qdec, kdin, kdst, cdec,
            row2(ret_norm_g), dw_kernel.reshape(CONV_KERNEL, CONV_BLOCKS, LANES),
            dw_bias.reshape(CONV_BLOCKS, LANES), row2(conv_ln_g), row2(conv_ln_b),
            row2(b_pw2), row2(post_ln_g), row2(post_ln_b))
    in_specs = [
        pl.BlockSpec((None, tile, D_MODEL), lambda b, s: (b, s, 0)),
        pl.BlockSpec(memory_space=pl.ANY), pl.BlockSpec(memory_space=pl.ANY), pl.BlockSpec(memory_space=pl.ANY),
        pl.BlockSpec((tile, LANES), lambda b, s: (s, 0)),
        pl.BlockSpec((tile, LANES), lambda b, s: (s, 0)),
    ] + [_const_spec(a.shape) for a in args[6:]]

    return pl.pallas_call(
        _layer_kernel,
        grid=(batch, seq // tile),
        in_specs=in_specs,
        out_specs=pl.BlockSpec((None, tile, D_MODEL), lambda b, s: (b, s, 0)),
        out_shape=jax.ShapeDtypeStruct(x.shape, x.dtype),
        scratch_shapes=[
            pltpu.VMEM((HEAD_PAIRS, LANES, RET_V_DIM), jnp.float32),
            pltpu.VMEM(((HALO + tile) * SUBLANES, LANES), jnp.float32),
            pltpu.VMEM((tile // SUBLANES, CONV_BLOCKS * SUBLANES, LANES), jnp.float32),
            pltpu.VMEM((tile, RET_WIDTH), jnp.float32),
            pltpu.VMEM(w_in.shape, jnp.bfloat16),
            pltpu.VMEM(w_pw2.shape, jnp.bfloat16),
            pltpu.VMEM(w_out.shape, jnp.bfloat16),
            pltpu.VMEM((STAGE_SLOTS, WIDE_STAGE_ROWS, w_in.shape[1]), jnp.float32),
            pltpu.VMEM((STAGE_SLOTS, NARROW_STAGE_ROWS, w_out.shape[1]), jnp.float32),
            pltpu.SemaphoreType.DMA((STAGE_SLOTS,)),
            pltpu.SemaphoreType.DMA((STAGE_SLOTS,)),
        ],
        compiler_params=pltpu.CompilerParams(
            dimension_semantics=("arbitrary", "arbitrary"),
            vmem_limit_bytes=VMEM_LIMIT_BYTES),
        name="hybrid_layer",
    )(*args)


@jax.jit
def kernel(x, w_in, ret_norm_g, dw_kernel, dw_bias, conv_ln_g, conv_ln_b, w_pw2, b_pw2, w_out, post_ln_g, post_ln_b):
    for layer in range(DEPTH):
        x = _layer(x, w_in[layer], ret_norm_g[layer], dw_kernel[layer], dw_bias[layer],
                   conv_ln_g[layer], conv_ln_b[layer], w_pw2[layer], b_pw2[layer],
                   w_out[layer], post_ln_g[layer], post_ln_b[layer])
    return x
```

```python
import numpy as np
import jax
import jax.numpy as jnp
from jax import lax
from jax.experimental import pallas as pl
from jax.experimental.pallas import tpu as pltpu

D_MODEL = 1024
RET_HEADS = 8
RET_QK_DIM = 64
RET_V_DIM = 128
RET_QK_WIDTH = RET_HEADS * RET_QK_DIM
RET_WIDTH = RET_HEADS * RET_V_DIM
CHUNK = 128
ROPE_THETA = 10000.0
CONV_WIDTH = D_MODEL
CONV_KERNEL = 31
LN_EPS = 1e-5
DEPTH = 1
DEEPNORM_ALPHA = (2.0 * DEPTH) ** 0.25

LANES = 128
SUBLANES = 8
HEAD_PAIRS = RET_HEADS // 2
CONV_BLOCKS = CONV_WIDTH // LANES
FIR_TAPS = 32
FIR_SPLITS = 4
HALO = FIR_TAPS
SEQ_TILE = 512
GLU_PARTS = 4
CONV_TAIL_PARTS = 2
TAIL_PARTS = 2
VMEM_LIMIT_BYTES = 56 * 1024 * 1024
STAGE_SLOTS = 4
WIDE_STAGE_ROWS = 64
NARROW_STAGE_ROWS = 256

_OFF = [int(v) for v in np.cumsum([0, RET_QK_WIDTH, RET_QK_WIDTH, RET_WIDTH, RET_WIDTH,
                                   CONV_WIDTH, CONV_WIDTH, CONV_WIDTH])]
Q, K, V, G_RET, A_CONV, GLU_CONV, G_CONV = range(7)


def _decay_tables():
    log_g = np.log1p(-np.exp2(-5.0 - np.arange(RET_HEADS, dtype=np.float64)))
    lane_head = np.arange(LANES) // RET_QK_DIM
    idx = np.arange(CHUNK, dtype=np.float64)
    scale = RET_QK_DIM ** -0.5
    qdec = np.zeros((HEAD_PAIRS, CHUNK, LANES))
    kdec_in = np.zeros((HEAD_PAIRS, CHUNK, LANES))
    kdec_st = np.zeros((HEAD_PAIRS, CHUNK, LANES))
    cdec = np.zeros((HEAD_PAIRS, LANES, RET_V_DIM))
    for p in range(HEAD_PAIRS):
        lg = log_g[2 * p + lane_head]
        qdec[p] = np.exp(lg[None, :] * (idx[:, None] + 1.0))
        kdec_in[p] = scale * np.exp(-lg[None, :] * (idx[:, None] + 1.0))
        kdec_st[p] = scale * np.exp(lg[None, :] * (CHUNK - 1.0 - idx[:, None]))
        cdec[p] = np.exp(lg * CHUNK)[:, None] * np.ones((1, RET_V_DIM))
    f = lambda a: jnp.asarray(a, jnp.float32)
    return f(qdec), f(kdec_in), f(kdec_st), f(cdec)


def _rope_tables(seq):
    half = RET_QK_DIM // 2
    inv_freq = ROPE_THETA ** (-np.arange(0, half, dtype=np.float64) * 2.0 / RET_QK_DIM)
    ang = np.arange(seq, dtype=np.float64)[:, None] * inv_freq[None, :]
    lane = np.arange(LANES)
    cos = np.cos(ang)[:, lane % half]
    sign = np.where((lane % RET_QK_DIM) < half, -1.0, 1.0)
    sin = np.sin(ang)[:, lane % half] * sign[None, :]
    return jnp.asarray(cos, jnp.float32), jnp.asarray(sin, jnp.float32)


def _sigmoid(x):
    return 0.5 * jnp.tanh(0.5 * x) + 0.5


def _silu(x):
    return x * _sigmoid(x)


def _layernorm_rows(x, g, b):
    mu = jnp.mean(x, axis=-1, keepdims=True)
    d = x - mu
    var = jnp.mean(d * d, axis=-1, keepdims=True)
    return d * lax.rsqrt(var + LN_EPS) * g + b


def _memo(fn):
    cache = {}

    def get(i):
        if i not in cache:
            cache[i] = fn(i)
        return cache[i]

    return get


def _tap_sum(a, b):
    if a is None:
        return b
    if b is None:
        return a
    return a + b


def _fast_fir(z, taps, splits):
    n_taps = len(taps)
    if splits == 0:
        assert any(tap is not None for tap in taps)

        def direct(k):
            acc = None
            for p, tap in enumerate(taps):
                if tap is not None:
                    term = z(k + n_taps - 1 - p) * tap
                    acc = term if acc is None else acc + term
            return acc
        return _memo(direct)

    even, odd = taps[0::2], taps[1::2]
    a = lambda m: z(2 * m + 1)
    b = lambda m: z(2 * m + 2)
    fa = _fast_fir(a, even, splits - 1)
    fb = _fast_fir(b, odd, splits - 1)
    fc = _fast_fir(_memo(lambda m: a(m) + b(m)), [_tap_sum(e, o) for e, o in zip(even, odd)], splits - 1)

    def combined(k):
        q = k // 2
        if k % 2 == 0:
            return fa(q) + fb(q - 1)
        return fc(q) - fa(q) - fb(q)

    return _memo(combined)


def _load_as_bf16(src_hbm, dst_ref, stage_ref, sems):
    n_slots, chunk = stage_ref.shape[0], stage_ref.shape[1]
    n_chunks = src_hbm.shape[0] // chunk

    def chunk_copy(i):
        slot = i % n_slots
        return pltpu.make_async_copy(src_hbm.at[pl.ds(i * chunk, chunk)], stage_ref.at[slot], sems.at[slot])

    for i in range(min(n_slots - 1, n_chunks)):
        chunk_copy(i).start()
    for i in range(n_chunks):
        ahead = i + n_slots - 1
        if ahead < n_chunks:
            chunk_copy(ahead).start()
        chunk_copy(i).wait()
        dst_ref[pl.ds(i * chunk, chunk), :] = stage_ref[i % n_slots].astype(jnp.bfloat16)


def _layer_kernel(x_ref, win_hbm, wpw_hbm, wout_hbm, cos_ref, sin_ref,
                  qdec_ref, kdin_ref, kdst_ref, cdec_ref,
                  retg_ref, dwk_ref, dwb_ref, clng_ref, clnb_ref, bpw_ref, plng_ref, plnb_ref,
                  o_ref, state_ref, ut_ref, cn_ref, ret_ref,
                  win_ref, wpw_ref, wout_ref, wide_stage, narrow_stage, wide_sems, narrow_sems):
    tile = x_ref.shape[0]
    groups = tile // SUBLANES
    f32, bf16 = jnp.float32, jnp.bfloat16

    @pl.when((pl.program_id(0) == 0) & (pl.program_id(1) == 0))
    def _():
        _load_as_bf16(win_hbm, win_ref, wide_stage, wide_sems)
        _load_as_bf16(wpw_hbm, wpw_ref, narrow_stage, narrow_sems)
        _load_as_bf16(wout_hbm, wout_ref, narrow_stage, narrow_sems)

    @pl.when(pl.program_id(1) == 0)
    def _():
        state_ref[...] = jnp.zeros_like(state_ref)
        ut_ref[0:HALO * SUBLANES, :] = jnp.zeros((HALO * SUBLANES, LANES), f32)

    xb = x_ref[...].astype(bf16)

    def proj(group):
        return jnp.dot(xb, win_ref[:, _OFF[group]:_OFF[group + 1]], preferred_element_type=f32)


    for part in range(GLU_PARTS):
        c0 = part * CONV_WIDTH // GLU_PARTS
        c1 = (part + 1) * CONV_WIDTH // GLU_PARTS
        a_part = jnp.dot(xb, win_ref[:, _OFF[A_CONV] + c0:_OFF[A_CONV] + c1], preferred_element_type=f32)
        gate = jnp.dot(xb, win_ref[:, _OFF[GLU_CONV] + c0:_OFF[GLU_CONV] + c1], preferred_element_type=f32)
        u = a_part * _sigmoid(gate)
        for cb in range(c0 // LANES, c1 // LANES):
            for tg in range(groups):
                ut_ref[pl.ds((HALO + tg * SUBLANES) * SUBLANES + cb, SUBLANES, stride=SUBLANES), :] = (
                    u[tg * SUBLANES:(tg + 1) * SUBLANES, cb * LANES - c0:(cb + 1) * LANES - c0])

    g_ret = proj(G_RET)
    g_conv = proj(G_CONV)

    cos = cos_ref[...]
    sin = sin_ref[...]
    lane = lax.broadcasted_iota(jnp.int32, (tile, LANES), 1)
    first_half = (lane % RET_QK_DIM) < (RET_QK_DIM // 2)

    def rotary(t):
        blocks = []
        for p in range(HEAD_PAIRS):
            blk = t[:, p * LANES:(p + 1) * LANES]
            swapped = jnp.where(first_half,
                                pltpu.roll(blk, LANES - RET_QK_DIM // 2, axis=1),
                                pltpu.roll(blk, RET_QK_DIM // 2, axis=1))
            blocks.append(blk * cos + swapped * sin)
        return blocks

    q_blocks = rotary(proj(Q))
    k_blocks = rotary(proj(K))
    v = proj(V).astype(bf16)

    row = lax.broadcasted_iota(jnp.int32, (CHUNK, CHUNK), 0)
    col = lax.broadcasted_iota(jnp.int32, (CHUNK, CHUNK), 1)
    head0_lane = col < RET_QK_DIM
    head0_row = row < RET_QK_DIM
    row2 = lax.broadcasted_iota(jnp.int32, (CHUNK, 2 * CHUNK), 0)
    col2 = lax.broadcasted_iota(jnp.int32, (CHUNK, 2 * CHUNK), 1)
    causal2 = row2 >= (col2 % CHUNK)
    head0_col = col2 < RET_V_DIM
    for c in range(tile // CHUNK):
        rows = slice(c * CHUNK, (c + 1) * CHUNK)
        for p in range(HEAD_PAIRS):
            qb = (q_blocks[p][rows] * qdec_ref[p]).astype(bf16)
            kp = k_blocks[p][rows]
            k_in = kp * kdin_ref[p]
            k_st = (kp * kdst_ref[p]).astype(bf16)
            k_heads = jnp.concatenate([jnp.where(head0_lane, k_in, 0.0),
                                       jnp.where(head0_lane, 0.0, k_in)], axis=0).astype(bf16)
            vp = v[rows, 2 * p * RET_V_DIM:(2 * p + 2) * RET_V_DIM]
            v_diag = jnp.concatenate([jnp.where(head0_col, vp, jnp.zeros_like(vp)),
                                      jnp.where(head0_col, jnp.zeros_like(vp), vp)], axis=0)
            state = state_ref[p]
            st_heads = jnp.concatenate([jnp.where(head0_row, state, 0.0),
                                        jnp.where(head0_row, 0.0, state)], axis=1).astype(bf16)
            s = lax.dot_general(qb, k_heads, (((1,), (1,)), ((), ())), preferred_element_type=f32)
            pm = jnp.where(causal2, s, 0.0).astype(bf16)
            o = (jnp.dot(pm, v_diag, preferred_element_type=f32)
                 + jnp.dot(qb, st_heads, preferred_element_type=f32))
            ret_ref[rows, 2 * p * RET_V_DIM:(2 * p + 2) * RET_V_DIM] = o
            kv = lax.dot_general(k_st, vp, (((0,), (0,)), ((), ())), preferred_element_type=f32)
            state_ref[p] = cdec_ref[p] * state + jnp.where(head0_row, kv[:, :RET_V_DIM], kv[:, RET_V_DIM:])

    taps = [dwk_ref[CONV_KERNEL - 1 - j] for j in range(CONV_KERNEL)] + [None] * (FIR_TAPS - CONV_KERNEL)
    conv_in = _memo(lambda i: ut_ref[pl.ds((i - (FIR_TAPS - 1) + HALO) * SUBLANES, SUBLANES), :])
    conv_out_t = _fast_fir(conv_in, taps, FIR_SPLITS)
    bias = dwb_ref[...]
    for t in range(tile):
        cn_ref[t // SUBLANES, pl.ds(t % SUBLANES, SUBLANES, stride=SUBLANES), :] = conv_out_t(t) + bias
    ut_ref[0:HALO * SUBLANES, :] = ut_ref[tile * SUBLANES:(tile + HALO) * SUBLANES, :]

    ret_parts = []
    for h in range(RET_HEADS):
        cols = slice(h * RET_V_DIM, (h + 1) * RET_V_DIM)
        y = ret_ref[:, cols]
        mu = jnp.mean(y, axis=-1, keepdims=True)
        d = y - mu
        var = jnp.mean(d * d, axis=-1, keepdims=True)
        yn = d * lax.rsqrt(var + LN_EPS) * retg_ref[:, cols]
        ret_parts.append((yn * _silu(g_ret[:, cols])).astype(bf16))
    ret_out = jnp.concatenate(ret_parts, axis=1)

    conv_parts = []
    for part in range(CONV_TAIL_PARTS):
        g0, g1 = part * groups // CONV_TAIL_PARTS, (part + 1) * groups // CONV_TAIL_PARTS
        rows = slice(g0 * SUBLANES, g1 * SUBLANES)
        cv = jnp.concatenate(
            [cn_ref[g0:g1, cb * SUBLANES:(cb + 1) * SUBLANES, :].reshape((g1 - g0) * SUBLANES, LANES)
             for cb in range(CONV_BLOCKS)], axis=1)
        cv = _silu(_layernorm_rows(cv, clng_ref[...], clnb_ref[...]))
        cv = jnp.dot(cv.astype(bf16), wpw_ref[...], preferred_element_type=f32) + bpw_ref[...]
        conv_parts.append((cv * _silu(g_conv[rows])).astype(bf16))
    conv_out = jnp.concatenate(conv_parts, axis=0)

    for part in range(TAIL_PARTS):
        rows = slice(part * tile // TAIL_PARTS, (part + 1) * tile // TAIL_PARTS)
        hmix = (jnp.dot(ret_out[rows], wout_ref[0:RET_WIDTH, :], preferred_element_type=f32)
                + jnp.dot(conv_out[rows], wout_ref[RET_WIDTH:RET_WIDTH + CONV_WIDTH, :], preferred_element_type=f32))
        o_ref[rows, :] = _layernorm_rows(DEEPNORM_ALPHA * x_ref[rows, :] + hmix, plng_ref[...], plnb_ref[...])


def _const_spec(shape):
    zeros = (0,) * len(shape)
    return pl.BlockSpec(shape, lambda b, s: zeros, pipeline_mode=pl.Buffered(1))


def _layer(x, w_in, ret_norm_g, dw_kernel, dw_bias, conv_ln_g, conv_ln_b,
           w_pw2, b_pw2, w_out, post_ln_g, post_ln_b):
    batch, seq, d = x.shape
    tile = SEQ_TILE
    assert d == D_MODEL and seq % tile == 0 and tile % CHUNK == 0
    assert CONV_KERNEL <= FIR_TAPS == 2 ** FIR_SPLITS * (FIR_TAPS >> FIR_SPLITS) and tile % 2 ** FIR_SPLITS == 0
    assert CONV_WIDTH % (GLU_PARTS * LANES) == 0 and tile % (SUBLANES * CONV_TAIL_PARTS) == 0 and tile % TAIL_PARTS == 0
    assert w_in.shape[0] % WIDE_STAGE_ROWS == 0 and w_pw2.shape[1] == w_out.shape[1]
    assert w_pw2.shape[0] % NARROW_STAGE_ROWS == 0 and w_out.shape[0] % NARROW_STAGE_ROWS == 0
    cos, sin = _rope_tables(seq)
    qdec, kdin, kdst, cdec = _decay_tables()
    row2 = lambda a: a.reshape(1, -1)

    args = (x, w_in, w_pw2, w_out, cos, sin,
            qdec, kdin, kdst, cdec,
            row2(ret_norm_g), dw_kernel.reshape(CONV_KERNEL, CONV_BLOCKS, LANES),
            dw_bias.reshape(CONV_BLOCKS, LANES), row2(conv_ln_g), row2(conv_ln_b),
            row2(b_pw2), row2(post_ln_g), row2(post_ln_b))
    in_specs = [
        pl.BlockSpec((None, tile, D_MODEL), lambda b, s: (b, s, 0)),
        pl.BlockSpec(memory_space=pl.ANY), pl.BlockSpec(memory_space=pl.ANY), pl.BlockSpec(memory_space=pl.ANY),
        pl.BlockSpec((tile, LANES), lambda b, s: (s, 0)),
        pl.BlockSpec((tile, LANES), lambda b, s: (s, 0)),
    ] + [_const_spec(a.shape) for a in args[6:]]

    return pl.pallas_call(
        _layer_kernel,
        grid=(batch, seq // tile),
        in_specs=in_specs,
        out_specs=pl.BlockSpec((None, tile, D_MODEL), lambda b, s: (b, s, 0)),
        out_shape=jax.ShapeDtypeStruct(x.shape, x.dtype),
        scratch_shapes=[
            pltpu.VMEM((HEAD_PAIRS, LANES, RET_V_DIM), jnp.float32),
            pltpu.VMEM(((HALO + tile) * SUBLANES, LANES), jnp.float32),
            pltpu.VMEM((tile // SUBLANES, CONV_BLOCKS * SUBLANES, LANES), jnp.float32),
            pltpu.VMEM((tile, RET_WIDTH), jnp.float32),
            pltpu.VMEM(w_in.shape, jnp.bfloat16),
            pltpu.VMEM(w_pw2.shape, jnp.bfloat16),
            pltpu.VMEM(w_out.shape, jnp.bfloat16),
            pltpu.VMEM((STAGE_SLOTS, WIDE_STAGE_ROWS, w_in.shape[1]), jnp.float32),
            pltpu.VMEM((STAGE_SLOTS, NARROW_STAGE_ROWS, w_out.shape[1]), jnp.float32),
            pltpu.SemaphoreType.DMA((STAGE_SLOTS,)),
            pltpu.SemaphoreType.DMA((STAGE_SLOTS,)),
        ],
        compiler_params=pltpu.CompilerParams(
            dimension_semantics=("arbitrary", "arbitrary"),
            vmem_limit_bytes=VMEM_LIMIT_BYTES),
        name="hybrid_layer",
    )(*args)


@jax.jit
def kernel(x, w_in, ret_norm_g, dw_kernel, dw_bias, conv_ln_g, conv_ln_b, w_pw2, b_pw2, w_out, post_ln_g, post_ln_b):
    for layer in range(DEPTH):
        x = _layer(x, w_in[layer], ret_norm_g[layer], dw_kernel[layer], dw_bias[layer],
                   conv_ln_g[layer], conv_ln_b[layer], w_pw2[layer], b_pw2[layer],
                   w_out[layer], post_ln_g[layer], post_ln_b[layer])
    return x
```

```python
import numpy as np
import jax
import jax.numpy as jnp
from jax import lax
from jax.experimental import pallas as pl
from jax.experimental.pallas import tpu as pltpu

D_MODEL = 1024
RET_HEADS = 8
RET_QK_DIM = 64
RET_V_DIM = 128
RET_QK_WIDTH = RET_HEADS * RET_QK_DIM
RET_WIDTH = RET_HEADS * RET_V_DIM
CHUNK = 128
ROPE_THETA = 10000.0
CONV_WIDTH = D_MODEL
CONV_KERNEL = 31
LN_EPS = 1e-5
DEPTH = 1
DEEPNORM_ALPHA = (2.0 * DEPTH) ** 0.25

LANES = 128
SUBLANES = 8
HEAD_PAIRS = RET_HEADS // 2
CONV_BLOCKS = CONV_WIDTH // LANES
FIR_TAPS = 32
FIR_SPLITS = 4
HALO = FIR_TAPS
SEQ_TILE = 512
GLU_PARTS = 4
CONV_TAIL_PARTS = 2
TAIL_PARTS = 2
VMEM_LIMIT_BYTES = 56 * 1024 * 1024
STAGE_SLOTS = 4
WIDE_STAGE_ROWS = 64
NARROW_STAGE_ROWS = 256

_OFF = [int(v) for v in np.cumsum([0, RET_QK_WIDTH, RET_QK_WIDTH, RET_WIDTH, RET_WIDTH,
                                   CONV_WIDTH, CONV_WIDTH, CONV_WIDTH])]
Q, K, V, G_RET, A_CONV, GLU_CONV, G_CONV = range(7)


def _decay_tables():
    log_g = np.log1p(-np.exp2(-5.0 - np.arange(RET_HEADS, dtype=np.float64)))
    lane_head = np.arange(LANES) // RET_QK_DIM
    idx = np.arange(CHUNK, dtype=np.float64)
    scale = RET_QK_DIM ** -0.5
    qdec = np.zeros((HEAD_PAIRS, CHUNK, LANES))
    kdec_in = np.zeros((HEAD_PAIRS, CHUNK, LANES))
    kdec_st = np.zeros((HEAD_PAIRS, CHUNK, LANES))
    cdec = np.zeros((HEAD_PAIRS, LANES, RET_V_DIM))
    for p in range(HEAD_PAIRS):
        lg = log_g[2 * p + lane_head]
        qdec[p] = np.exp(lg[None, :] * (idx[:, None] + 1.0))
        kdec_in[p] = scale * np.exp(-lg[None, :] * (idx[:, None] + 1.0))
        kdec_st[p] = scale * np.exp(lg[None, :] * (CHUNK - 1.0 - idx[:, None]))
        cdec[p] = np.exp(lg * CHUNK)[:, None] * np.ones((1, RET_V_DIM))
    f = lambda a: jnp.asarray(a, jnp.float32)
    return f(qdec), f(kdec_in), f(kdec_st), f(cdec)


def _rope_tables(seq):
    half = RET_QK_DIM // 2
    inv_freq = ROPE_THETA ** (-np.arange(0, half, dtype=np.float64) * 2.0 / RET_QK_DIM)
    ang = np.arange(seq, dtype=np.float64)[:, None] * inv_freq[None, :]
    lane = np.arange(LANES)
    cos = np.cos(ang)[:, lane % half]
    sign = np.where((lane % RET_QK_DIM) < half, -1.0, 1.0)
    sin = np.sin(ang)[:, lane % half] * sign[None, :]
    return jnp.asarray(cos, jnp.float32), jnp.asarray(sin, jnp.float32)


def _sigmoid(x):
    return 0.5 * jnp.tanh(0.5 * x) + 0.5


def _silu(x):
    return x * _sigmoid(x)


def _layernorm_rows(x, g, b):
    mu = jnp.mean(x, axis=-1, keepdims=True)
    d = x - mu
    var = jnp.mean(d * d, axis=-1, keepdims=True)
    return d * lax.rsqrt(var + LN_EPS) * g + b


def _memo(fn):
    cache = {}

    def get(i):
        if i not in cache:
            cache[i] = fn(i)
        return cache[i]

    return get


def _tap_sum(a, b):
    if a is None:
        return b
    if b is None:
        return a
    return a + b


def _fast_fir(z, taps, splits):
    n_taps = len(taps)
    if splits == 0:
        assert any(tap is not None for tap in taps)

        def direct(k):
            acc = None
            for p, tap in enumerate(taps):
                if tap is not None:
                    term = z(k + n_taps - 1 - p) * tap
                    acc = term if acc is None else acc + term
            return acc
        return _memo(direct)

    even, odd = taps[0::2], taps[1::2]
    a = lambda m: z(2 * m + 1)
    b = lambda m: z(2 * m + 2)
    fa = _fast_fir(a, even, splits - 1)
    fb = _fast_fir(b, odd, splits - 1)
    fc = _fast_fir(_memo(lambda m: a(m) + b(m)), [_tap_sum(e, o) for e, o in zip(even, odd)], splits - 1)

    def combined(k):
        q = k // 2
        if k % 2 == 0:
            return fa(q) + fb(q - 1)
        return fc(q) - fa(q) - fb(q)

    return _memo(combined)


def _load_as_bf16(src_hbm, dst_ref, stage_ref, sems):
    n_slots, chunk = stage_ref.shape[0], stage_ref.shape[1]
    n_chunks = src_hbm.shape[0] // chunk

    def chunk_copy(i):
        slot = i % n_slots
        return pltpu.make_async_copy(src_hbm.at[pl.ds(i * chunk, chunk)], stage_ref.at[slot], sems.at[slot])

    for i in range(min(n_slots - 1, n_chunks)):
        chunk_copy(i).start()
    for i in range(n_chunks):
        ahead = i + n_slots - 1
        if ahead < n_chunks:
            chunk_copy(ahead).start()
        chunk_copy(i).wait()
        dst_ref[pl.ds(i * chunk, chunk), :] = stage_ref[i % n_slots].astype(jnp.bfloat16)


def _layer_kernel(x_ref, win_hbm, wpw_hbm, wout_hbm, cos_ref, sin_ref,
                  qdec_ref, kdin_ref, kdst_ref, cdec_ref,
                  retg_ref, dwk_ref, dwb_ref, clng_ref, clnb_ref, bpw_ref, plng_ref, plnb_ref,
                  o_ref, state_ref, ut_ref, cn_ref, ret_ref,
                  win_ref, wpw_ref, wout_ref, wide_stage, narrow_stage, wide_sems, narrow_sems):
    tile = x_ref.shape[0]
    groups = tile // SUBLANES
    f32, bf16 = jnp.float32, jnp.bfloat16

    @pl.when((pl.program_id(0) == 0) & (pl.program_id(1) == 0))
    def _():
        _load_as_bf16(win_hbm, win_ref, wide_stage, wide_sems)
        _load_as_bf16(wpw_hbm, wpw_ref, narrow_stage, narrow_sems)
        _load_as_bf16(wout_hbm, wout_ref, narrow_stage, narrow_sems)

    @pl.when(pl.program_id(1) == 0)
    def _():
        state_ref[...] = jnp.zeros_like(state_ref)
        ut_ref[0:HALO * SUBLANES, :] = jnp.zeros((HALO * SUBLANES, LANES), f32)

    xb = x_ref[...].astype(bf16)

    def proj(group):
        return jnp.dot(xb, win_ref[:, _OFF[group]:_OFF[group + 1]], preferred_element_type=f32)


    for part in range(GLU_PARTS):
        c0 = part * CONV_WIDTH // GLU_PARTS
        c1 = (part + 1) * CONV_WIDTH // GLU_PARTS
        a_part = jnp.dot(xb, win_ref[:, _OFF[A_CONV] + c0:_OFF[A_CONV] + c1], preferred_element_type=f32)
        gate = jnp.dot(xb, win_ref[:, _OFF[GLU_CONV] + c0:_OFF[GLU_CONV] + c1], preferred_element_type=f32)
        u = a_part * _sigmoid(gate)
        for cb in range(c0 // LANES, c1 // LANES):
            for tg in range(groups):
                ut_ref[pl.ds((HALO + tg * SUBLANES) * SUBLANES + cb, SUBLANES, stride=SUBLANES), :] = (
                    u[tg * SUBLANES:(tg + 1) * SUBLANES, cb * LANES - c0:(cb + 1) * LANES - c0])

    g_ret = proj(G_RET)
    g_conv = proj(G_CONV)

    cos = cos_ref[...]
    sin = sin_ref[...]
    lane = lax.broadcasted_iota(jnp.int32, (tile, LANES), 1)
    first_half = (lane % RET_QK_DIM) < (RET_QK_DIM // 2)

    def rotary(t):
        blocks = []
        for p in range(HEAD_PAIRS):
            blk = t[:, p * LANES:(p + 1) * LANES]
            swapped = jnp.where(first_half,
                                pltpu.roll(blk, LANES - RET_QK_DIM // 2, axis=1),
                                pltpu.roll(blk, RET_QK_DIM // 2, axis=1))
            blocks.append(blk * cos + swapped * sin)
        return blocks

    q_blocks = rotary(proj(Q))
    k_blocks = rotary(proj(K))
    v = proj(V).astype(bf16)

    row = lax.broadcasted_iota(jnp.int32, (CHUNK, CHUNK), 0)
    col = lax.broadcasted_iota(jnp.int32, (CHUNK, CHUNK), 1)
    head0_lane = col < RET_QK_DIM
    head0_row = row < RET_QK_DIM
    row2 = lax.broadcasted_iota(jnp.int32, (CHUNK, 2 * CHUNK), 0)
    col2 = lax.broadcasted_iota(jnp.int32, (CHUNK, 2 * CHUNK), 1)
    causal2 = row2 >= (col2 % CHUNK)
    head0_col = col2 < RET_V_DIM
    for p in range(HEAD_PAIRS):
        state = state_ref[p]
        for c in range(tile // CHUNK):
            rows = slice(c * CHUNK, (c + 1) * CHUNK)
            qb = (q_blocks[p][rows] * qdec_ref[p]).astype(bf16)
            kp = k_blocks[p][rows]
            k_in = kp * kdin_ref[p]
            k_st = (kp * kdst_ref[p]).astype(bf16)
            k_heads = jnp.concatenate([jnp.where(head0_lane, k_in, 0.0),
                                       jnp.where(head0_lane, 0.0, k_in)], axis=0).astype(bf16)
            vp = v[rows, 2 * p * RET_V_DIM:(2 * p + 2) * RET_V_DIM]
            v_diag = jnp.concatenate([jnp.where(head0_col, vp, jnp.zeros_like(vp)),
                                      jnp.where(head0_col, jnp.zeros_like(vp), vp)], axis=0)
            st_heads = jnp.concatenate([jnp.where(head0_row, state, 0.0),
                                        jnp.where(head0_row, 0.0, state)], axis=1).astype(bf16)
            s = lax.dot_general(qb, k_heads, (((1,), (1,)), ((), ())), preferred_element_type=f32)
            pm = jnp.where(causal2, s, 0.0).astype(bf16)
            o = (jnp.dot(pm, v_diag, preferred_element_type=f32)
                 + jnp.dot(qb, st_heads, preferred_element_type=f32))
            ret_ref[rows, 2 * p * RET_V_DIM:(2 * p + 2) * RET_V_DIM] = o
            kv = lax.dot_general(k_st, vp, (((0,), (0,)), ((), ())), preferred_element_type=f32)
            state = cdec_ref[p] * state + jnp.where(head0_row, kv[:, :RET_V_DIM], kv[:, RET_V_DIM:])
        state_ref[p] = state

    taps = [dwk_ref[CONV_KERNEL - 1 - j] for j in range(CONV_KERNEL)] + [None] * (FIR_TAPS - CONV_KERNEL)
    conv_in = _memo(lambda i: ut_ref[pl.ds((i - (FIR_TAPS - 1) + HALO) * SUBLANES, SUBLANES), :])
    conv_out_t = _fast_fir(conv_in, taps, FIR_SPLITS)
    bias = dwb_ref[...]
    for t in range(tile):
        cn_ref[t // SUBLANES, pl.ds(t % SUBLANES, SUBLANES, stride=SUBLANES), :] = conv_out_t(t) + bias
    ut_ref[0:HALO * SUBLANES, :] = ut_ref[tile * SUBLANES:(tile + HALO) * SUBLANES, :]

    ret_parts = []
    for h in range(RET_HEADS):
        cols = slice(h * RET_V_DIM, (h + 1) * RET_V_DIM)
        y = ret_ref[:, cols]
        mu = jnp.mean(y, axis=-1, keepdims=True)
        d = y - mu
        var = jnp.mean(d * d, axis=-1, keepdims=True)
        yn = d * lax.rsqrt(var + LN_EPS) * retg_ref[:, cols]
        ret_parts.append((yn * _silu(g_ret[:, cols])).astype(bf16))
    ret_out = jnp.concatenate(ret_parts, axis=1)

    conv_parts = []
    for part in range(CONV_TAIL_PARTS):
        g0, g1 = part * groups // CONV_TAIL_PARTS, (part + 1) * groups // CONV_TAIL_PARTS
        rows = slice(g0 * SUBLANES, g1 * SUBLANES)
        cv = jnp.concatenate(
            [cn_ref[g0:g1, cb * SUBLANES:(cb + 1) * SUBLANES, :].reshape((g1 - g0) * SUBLANES, LANES)
             for cb in range(CONV_BLOCKS)], axis=1)
        cv = _silu(_layernorm_rows(cv, clng_ref[...], clnb_ref[...]))
        cv = jnp.dot(cv.astype(bf16), wpw_ref[...], preferred_element_type=f32) + bpw_ref[...]
        conv_parts.append((cv * _silu(g_conv[rows])).astype(bf16))
    conv_out = jnp.concatenate(conv_parts, axis=0)

    for part in range(TAIL_PARTS):
        rows = slice(part * tile // TAIL_PARTS, (part + 1) * tile // TAIL_PARTS)
        hmix = (jnp.dot(ret_out[rows], wout_ref[0:RET_WIDTH, :], preferred_element_type=f32)
                + jnp.dot(conv_out[rows], wout_ref[RET_WIDTH:RET_WIDTH + CONV_WIDTH, :], preferred_element_type=f32))
        o_ref[rows, :] = _layernorm_rows(DEEPNORM_ALPHA * x_ref[rows, :] + hmix, plng_ref[...], plnb_ref[...])


def _const_spec(shape):
    zeros = (0,) * len(shape)
    return pl.BlockSpec(shape, lambda b, s: zeros, pipeline_mode=pl.Buffered(1))


def _layer(x, w_in, ret_norm_g, dw_kernel, dw_bias, conv_ln_g, conv_ln_b,
           w_pw2, b_pw2, w_out, post_ln_g, post_ln_b):
    batch, seq, d = x.shape
    tile = SEQ_TILE
    assert d == D_MODEL and seq % tile == 0 and tile % CHUNK == 0
    assert CONV_KERNEL <= FIR_TAPS == 2 ** FIR_SPLITS * (FIR_TAPS >> FIR_SPLITS) and tile % 2 ** FIR_SPLITS == 0
    assert CONV_WIDTH % (GLU_PARTS * LANES) == 0 and tile % (SUBLANES * CONV_TAIL_PARTS) == 0 and tile % TAIL_PARTS == 0
    assert w_in.shape[0] % WIDE_STAGE_ROWS == 0 and w_pw2.shape[1] == w_out.shape[1]
    assert w_pw2.shape[0] % NARROW_STAGE_ROWS == 0 and w_out.shape[0] % NARROW_STAGE_ROWS == 0
    cos, sin = _rope_tables(seq)
    qdec, kdin, kdst, cdec = _decay_tables()
    row2 = lambda a: a.reshape(1, -1)

    args = (x, w_in, w_pw2, w_out, cos, sin,
            qdec, kdin, kdst, cdec,
            row2(ret_norm_g), dw_kernel.reshape(CONV_KERNEL, CONV_BLOCKS, LANES),
            dw_bias.reshape(CONV_BLOCKS, LANES), row2(conv_ln_g), row2(conv_ln_b),
            row2(b_pw2), row2(post_ln_g), row2(post_ln_b))
    in_specs = [
        pl.BlockSpec((None, tile, D_MODEL), lambda b, s: (b, s, 0)),
        pl.BlockSpec(memory_space=pl.ANY), pl.BlockSpec(memory_space=pl.ANY), pl.BlockSpec(memory_space=pl.ANY),
        pl.BlockSpec((tile, LANES), lambda b, s: (s, 0)),
        pl.BlockSpec((tile, LANES), lambda b, s: (s, 0)),
    ] + [_const_spec(a.shape) for a in args[6:]]

    return pl.pallas_call(
        _layer_kernel,
        grid=(batch, seq // tile),
        in_specs=in_specs,
        out_specs=pl.BlockSpec((None, tile, D_MODEL), lambda b, s: (b, s, 0)),
        out_shape=jax.ShapeDtypeStruct(x.shape, x.dtype),
        scratch_shapes=[
            pltpu.VMEM((HEAD_PAIRS, LANES, RET_V_DIM), jnp.float32),
            pltpu.VMEM(((HALO + tile) * SUBLANES, LANES), jnp.float32),
            pltpu.VMEM((tile // SUBLANES, CONV_BLOCKS * SUBLANES, LANES), jnp.float32),
            pltpu.VMEM((tile, RET_WIDTH), jnp.float32),
            pltpu.VMEM(w_in.shape, jnp.bfloat16),
            pltpu.VMEM(w_pw2.shape, jnp.bfloat16),
            pltpu.VMEM(w_out.shape, jnp.bfloat16),
            pltpu.VMEM((STAGE_SLOTS, WIDE_STAGE_ROWS, w_in.shape[1]), jnp.float32),
            pltpu.VMEM((STAGE_SLOTS, NARROW_STAGE_ROWS, w_out.shape[1]), jnp.float32),
            pltpu.SemaphoreType.DMA((STAGE_SLOTS,)),
            pltpu.SemaphoreType.DMA((STAGE_SLOTS,)),
        ],
        compiler_params=pltpu.CompilerParams(
            dimension_semantics=("arbitrary", "arbitrary"),
            vmem_limit_bytes=VMEM_LIMIT_BYTES),
        name="hybrid_layer",
    )(*args)


@jax.jit
def kernel(x, w_in, ret_norm_g, dw_kernel, dw_bias, conv_ln_g, conv_ln_b, w_pw2, b_pw2, w_out, post_ln_g, post_ln_b):
    for layer in range(DEPTH):
        x = _layer(x, w_in[layer], ret_norm_g[layer], dw_kernel[layer], dw_bias[layer],
                   conv_ln_g[layer], conv_ln_b[layer], w_pw2[layer], b_pw2[layer],
                   w_out[layer], post_ln_g[layer], post_ln_b[layer])
    return x
```

```python
import numpy as np
import jax
import jax.numpy as jnp
from jax import lax
from jax.experimental import pallas as pl
from jax.experimental.pallas import tpu as pltpu

D_MODEL = 1024
RET_HEADS = 8
RET_QK_DIM = 64
RET_V_DIM = 128
RET_QK_WIDTH = RET_HEADS * RET_QK_DIM
RET_WIDTH = RET_HEADS * RET_V_DIM
CHUNK = 128
ROPE_THETA = 10000.0
CONV_WIDTH = D_MODEL
CONV_KERNEL = 31
LN_EPS = 1e-5
DEPTH = 1
DEEPNORM_ALPHA = (2.0 * DEPTH) ** 0.25

LANES = 128
SUBLANES = 8
HEAD_PAIRS = RET_HEADS // 2
CONV_BLOCKS = CONV_WIDTH // LANES
FIR_TAPS = 32
FIR_SPLITS = 4
HALO = FIR_TAPS
SEQ_TILE = 512
GLU_PARTS = 4
CONV_TAIL_PARTS = 2
TAIL_PARTS = 2
VMEM_LIMIT_BYTES = 56 * 1024 * 1024
STAGE_SLOTS = 4
WIDE_STAGE_ROWS = 64
NARROW_STAGE_ROWS = 256

_OFF = [int(v) for v in np.cumsum([0, RET_QK_WIDTH, RET_QK_WIDTH, RET_WIDTH, RET_WIDTH,
                                   CONV_WIDTH, CONV_WIDTH, CONV_WIDTH])]
Q, K, V, G_RET, A_CONV, GLU_CONV, G_CONV = range(7)


def _decay_tables():
    log_g = np.log1p(-np.exp2(-5.0 - np.arange(RET_HEADS, dtype=np.float64)))
    lane_head = np.arange(LANES) // RET_QK_DIM
    idx = np.arange(CHUNK, dtype=np.float64)
    scale = RET_QK_DIM ** -0.5
    qdec = np.zeros((HEAD_PAIRS, CHUNK, LANES))
    kdec_in = np.zeros((HEAD_PAIRS, CHUNK, LANES))
    kdec_st = np.zeros((HEAD_PAIRS, CHUNK, LANES))
    cdec = np.zeros((HEAD_PAIRS, LANES, RET_V_DIM))
    for p in range(HEAD_PAIRS):
        lg = log_g[2 * p + lane_head]
        qdec[p] = np.exp(lg[None, :] * (idx[:, None] + 1.0))
        kdec_in[p] = scale * np.exp(-lg[None, :] * (idx[:, None] + 1.0))
        kdec_st[p] = scale * np.exp(lg[None, :] * (CHUNK - 1.0 - idx[:, None]))
        cdec[p] = np.exp(lg * CHUNK)[:, None] * np.ones((1, RET_V_DIM))
    f = lambda a: jnp.asarray(a, jnp.float32)
    return f(qdec), f(kdec_in), f(kdec_st), f(cdec)


def _rope_tables(seq):
    half = RET_QK_DIM // 2
    inv_freq = ROPE_THETA ** (-np.arange(0, half, dtype=np.float64) * 2.0 / RET_QK_DIM)
    ang = np.arange(seq, dtype=np.float64)[:, None] * inv_freq[None, :]
    lane = np.arange(LANES)
    cos = np.cos(ang)[:, lane % half]
    sign = np.where((lane % RET_QK_DIM) < half, -1.0, 1.0)
    sin = np.sin(ang)[:, lane % half] * sign[None, :]
    return jnp.asarray(cos, jnp.float32), jnp.asarray(sin, jnp.float32)


def _sigmoid(x):
    return 0.5 * jnp.tanh(0.5 * x) + 0.5


def _silu(x):
    return x * _sigmoid(x)


def _layernorm_rows(x, g, b):
    mu = jnp.mean(x, axis=-1, keepdims=True)
    d = x - mu
    var = jnp.mean(d * d, axis=-1, keepdims=True)
    return d * lax.rsqrt(var + LN_EPS) * g + b


def _memo(fn):
    cache = {}

    def get(i):
        if i not in cache:
            cache[i] = fn(i)
        return cache[i]

    return get


def _tap_sum(a, b):
    if a is None:
        return b
    if b is None:
        return a
    return a + b


def _fast_fir(z, taps, splits):
    n_taps = len(taps)
    if splits == 0:
        assert any(tap is not None for tap in taps)

        def direct(k):
            acc = None
            for p, tap in enumerate(taps):
                if tap is not None:
                    term = z(k + n_taps - 1 - p) * tap
                    acc = term if acc is None else acc + term
            return acc
        return _memo(direct)

    even, odd = taps[0::2], taps[1::2]
    a = lambda m: z(2 * m + 1)
    b = lambda m: z(2 * m + 2)
    fa = _fast_fir(a, even, splits - 1)
    fb = _fast_fir(b, odd, splits - 1)
    fc = _fast_fir(_memo(lambda m: a(m) + b(m)), [_tap_sum(e, o) for e, o in zip(even, odd)], splits - 1)

    def combined(k):
        q = k // 2
        if k % 2 == 0:
            return fa(q) + fb(q - 1)
        return fc(q) - fa(q) - fb(q)

    return _memo(combined)


def _load_as_bf16(src_hbm, dst_ref, stage_ref, sems):
    n_slots, chunk = stage_ref.shape[0], stage_ref.shape[1]
    n_chunks = src_hbm.shape[0] // chunk

    def chunk_copy(i):
        slot = i % n_slots
        return pltpu.make_async_copy(src_hbm.at[pl.ds(i * chunk, chunk)], stage_ref.at[slot], sems.at[slot])

    for i in range(min(n_slots - 1, n_chunks)):
        chunk_copy(i).start()
    for i in range(n_chunks):
        ahead = i + n_slots - 1
        if ahead < n_chunks:
            chunk_copy(ahead).start()
        chunk_copy(i).wait()
        dst_ref[pl.ds(i * chunk, chunk), :] = stage_ref[i % n_slots].astype(jnp.bfloat16)


def _layer_kernel(x_ref, win_hbm, wpw_hbm, wout_hbm, cos_ref, sin_ref,
                  qdec_ref, kdin_ref, kdst_ref, cdec_ref,
                  retg_ref, dwk_ref, dwb_ref, clng_ref, clnb_ref, bpw_ref, plng_ref, plnb_ref,
                  o_ref, state_ref, ut_ref, cn_ref, ret_ref,
                  win_ref, wpw_ref, wout_ref, wide_stage, narrow_stage, wide_sems, narrow_sems):
    tile = x_ref.shape[0]
    groups = tile // SUBLANES
    f32, bf16 = jnp.float32, jnp.bfloat16

    @pl.when((pl.program_id(0) == 0) & (pl.program_id(1) == 0))
    def _():
        _load_as_bf16(win_hbm, win_ref, wide_stage, wide_sems)
        _load_as_bf16(wpw_hbm, wpw_ref, narrow_stage, narrow_sems)
        _load_as_bf16(wout_hbm, wout_ref, narrow_stage, narrow_sems)

    @pl.when(pl.program_id(1) == 0)
    def _():
        state_ref[...] = jnp.zeros_like(state_ref)
        ut_ref[0:HALO * SUBLANES, :] = jnp.zeros((HALO * SUBLANES, LANES), f32)

    xb = x_ref[...].astype(bf16)

    def proj(group):
        return jnp.dot(xb, win_ref[:, _OFF[group]:_OFF[group + 1]], preferred_element_type=f32)


    for part in range(GLU_PARTS):
        c0 = part * CONV_WIDTH // GLU_PARTS
        c1 = (part + 1) * CONV_WIDTH // GLU_PARTS
        a_part = jnp.dot(xb, win_ref[:, _OFF[A_CONV] + c0:_OFF[A_CONV] + c1], preferred_element_type=f32)
        gate = jnp.dot(xb, win_ref[:, _OFF[GLU_CONV] + c0:_OFF[GLU_CONV] + c1], preferred_element_type=f32)
        u = a_part * _sigmoid(gate)
        for cb in range(c0 // LANES, c1 // LANES):
            for tg in range(groups):
                ut_ref[pl.ds((HALO + tg * SUBLANES) * SUBLANES + cb, SUBLANES, stride=SUBLANES), :] = (
                    u[tg * SUBLANES:(tg + 1) * SUBLANES, cb * LANES - c0:(cb + 1) * LANES - c0])

    g_ret = proj(G_RET)
    g_conv = proj(G_CONV)

    cos = cos_ref[...]
    sin = sin_ref[...]
    lane = lax.broadcasted_iota(jnp.int32, (tile, LANES), 1)
    first_half = (lane % RET_QK_DIM) < (RET_QK_DIM // 2)

    def rotary(t):
        blocks = []
        for p in range(HEAD_PAIRS):
            blk = t[:, p * LANES:(p + 1) * LANES]
            swapped = jnp.where(first_half,
                                pltpu.roll(blk, LANES - RET_QK_DIM // 2, axis=1),
                                pltpu.roll(blk, RET_QK_DIM // 2, axis=1))
            blocks.append(blk * cos + swapped * sin)
        return blocks

    q_blocks = rotary(proj(Q))
    k_blocks = rotary(proj(K))
    v = proj(V).astype(bf16)

    row = lax.broadcasted_iota(jnp.int32, (CHUNK, CHUNK), 0)
    col = lax.broadcasted_iota(jnp.int32, (CHUNK, CHUNK), 1)
    head0_lane = col < RET_QK_DIM
    head0_row = row < RET_QK_DIM
    row2 = lax.broadcasted_iota(jnp.int32, (CHUNK, 2 * CHUNK), 0)
    col2 = lax.broadcasted_iota(jnp.int32, (CHUNK, 2 * CHUNK), 1)
    causal2 = row2 >= (col2 % CHUNK)
    head0_col = col2 < RET_V_DIM
    states = [state_ref[p] for p in range(HEAD_PAIRS)]
    for c in range(tile // CHUNK):
        rows = slice(c * CHUNK, (c + 1) * CHUNK)
        for p in range(HEAD_PAIRS):
            qb = (q_blocks[p][rows] * qdec_ref[p]).astype(bf16)
            kp = k_blocks[p][rows]
            k_in = kp * kdin_ref[p]
            k_st = (kp * kdst_ref[p]).astype(bf16)
            k_heads = jnp.concatenate([jnp.where(head0_lane, k_in, 0.0),
                                       jnp.where(head0_lane, 0.0, k_in)], axis=0).astype(bf16)
            vp = v[rows, 2 * p * RET_V_DIM:(2 * p + 2) * RET_V_DIM]
            v_diag = jnp.concatenate([jnp.where(head0_col, vp, jnp.zeros_like(vp)),
                                      jnp.where(head0_col, jnp.zeros_like(vp), vp)], axis=0)
            state = states[p]
            st_heads = jnp.concatenate([jnp.where(head0_row, state, 0.0),
                                        jnp.where(head0_row, 0.0, state)], axis=1).astype(bf16)
            s = lax.dot_general(qb, k_heads, (((1,), (1,)), ((), ())), preferred_element_type=f32)
            pm = jnp.where(causal2, s, 0.0).astype(bf16)
            o = (jnp.dot(pm, v_diag, preferred_element_type=f32)
                 + jnp.dot(qb, st_heads, preferred_element_type=f32))
            ret_ref[rows, 2 * p * RET_V_DIM:(2 * p + 2) * RET_V_DIM] = o
            kv = lax.dot_general(k_st, vp, (((0,), (0,)), ((), ())), preferred_element_type=f32)
            states[p] = cdec_ref[p] * state + jnp.where(head0_row, kv[:, :RET_V_DIM], kv[:, RET_V_DIM:])
    for p in range(HEAD_PAIRS):
        state_ref[p] = states[p]

    taps = [dwk_ref[CONV_KERNEL - 1 - j] for j in range(CONV_KERNEL)] + [None] * (FIR_TAPS - CONV_KERNEL)
    conv_in = _memo(lambda i: ut_ref[pl.ds((i - (FIR_TAPS - 1) + HALO) * SUBLANES, SUBLANES), :])
    conv_out_t = _fast_fir(conv_in, taps, FIR_SPLITS)
    bias = dwb_ref[...]
    for t in range(tile):
        cn_ref[t // SUBLANES, pl.ds(t % SUBLANES, SUBLANES, stride=SUBLANES), :] = conv_out_t(t) + bias
    ut_ref[0:HALO * SUBLANES, :] = ut_ref[tile * SUBLANES:(tile + HALO) * SUBLANES, :]

    ret_parts = []
    for h in range(RET_HEADS):
        cols = slice(h * RET_V_DIM, (h + 1) * RET_V_DIM)
        y = ret_ref[:, cols]
        mu = jnp.mean(y, axis=-1, keepdims=True)
        d = y - mu
        var = jnp.mean(d * d, axis=-1, keepdims=True)
        yn = d * lax.rsqrt(var + LN_EPS) * retg_ref[:, cols]
        ret_parts.append((yn * _silu(g_ret[:, cols])).astype(bf16))
    ret_out = jnp.concatenate(ret_parts, axis=1)

    conv_parts = []
    for part in range(CONV_TAIL_PARTS):
        g0, g1 = part * groups // CONV_TAIL_PARTS, (part + 1) * groups // CONV_TAIL_PARTS
        rows = slice(g0 * SUBLANES, g1 * SUBLANES)
        cv = jnp.concatenate(
            [cn_ref[g0:g1, cb * SUBLANES:(cb + 1) * SUBLANES, :].reshape((g1 - g0) * SUBLANES, LANES)
             for cb in range(CONV_BLOCKS)], axis=1)
        cv = _silu(_layernorm_rows(cv, clng_ref[...], clnb_ref[...]))
        cv = jnp.dot(cv.astype(bf16), wpw_ref[...], preferred_element_type=f32) + bpw_ref[...]
        conv_parts.append((cv * _silu(g_conv[rows])).astype(bf16))
    conv_out = jnp.concatenate(conv_parts, axis=0)

    for part in range(TAIL_PARTS):
        rows = slice(part * tile // TAIL_PARTS, (part + 1) * tile // TAIL_PARTS)
        hmix = (jnp.dot(ret_out[rows], wout_ref[0:RET_WIDTH, :], preferred_element_type=f32)
                + jnp.dot(conv_out[rows], wout_ref[RET_WIDTH:RET_WIDTH + CONV_WIDTH, :], preferred_element_type=f32))
        o_ref[rows, :] = _layernorm_rows(DEEPNORM_ALPHA * x_ref[rows, :] + hmix, plng_ref[...], plnb_ref[...])


def _const_spec(shape):
    zeros = (0,) * len(shape)
    return pl.BlockSpec(shape, lambda b, s: zeros, pipeline_mode=pl.Buffered(1))


def _layer(x, w_in, ret_norm_g, dw_kernel, dw_bias, conv_ln_g, conv_ln_b,
           w_pw2, b_pw2, w_out, post_ln_g, post_ln_b):
    batch, seq, d = x.shape
    tile = SEQ_TILE
    assert d == D_MODEL and seq % tile == 0 and tile % CHUNK == 0
    assert CONV_KERNEL <= FIR_TAPS == 2 ** FIR_SPLITS * (FIR_TAPS >> FIR_SPLITS) and tile % 2 ** FIR_SPLITS == 0
    assert CONV_WIDTH % (GLU_PARTS * LANES) == 0 and tile % (SUBLANES * CONV_TAIL_PARTS) == 0 and tile % TAIL_PARTS == 0
    assert w_in.shape[0] % WIDE_STAGE_ROWS == 0 and w_pw2.shape[1] == w_out.shape[1]
    assert w_pw2.shape[0] % NARROW_STAGE_ROWS == 0 and w_out.shape[0] % NARROW_STAGE_ROWS == 0
    cos, sin = _rope_tables(seq)
    qdec, kdin, kdst, cdec = _decay_tables()
    row2 = lambda a: a.reshape(1, -1)

    args = (x, w_in, w_pw2, w_out, cos, sin,
            qdec, kdin, kdst, cdec,
            row2(ret_norm_g), dw_kernel.reshape(CONV_KERNEL, CONV_BLOCKS, LANES),
            dw_bias.reshape(CONV_BLOCKS, LANES), row2(conv_ln_g), row2(conv_ln_b),
            row2(b_pw2), row2(post_ln_g), row2(post_ln_b))
    in_specs = [
        pl.BlockSpec((None, tile, D_MODEL), lambda b, s: (b, s, 0)),
        pl.BlockSpec(memory_space=pl.ANY), pl.BlockSpec(memory_space=pl.ANY), pl.BlockSpec(memory_space=pl.ANY),
        pl.BlockSpec((tile, LANES), lambda b, s: (s, 0)),
        pl.BlockSpec((tile, LANES), lambda b, s: (s, 0)),
    ] + [_const_spec(a.shape) for a in args[6:]]

    return pl.pallas_call(
        _layer_kernel,
        grid=(batch, seq // tile),
        in_specs=in_specs,
        out_specs=pl.BlockSpec((None, tile, D_MODEL), lambda b, s: (b, s, 0)),
        out_shape=jax.ShapeDtypeStruct(x.shape, x.dtype),
        scratch_shapes=[
            pltpu.VMEM((HEAD_PAIRS, LANES, RET_V_DIM), jnp.float32),
            pltpu.VMEM(((HALO + tile) * SUBLANES, LANES), jnp.float32),
            pltpu.VMEM((tile // SUBLANES, CONV_BLOCKS * SUBLANES, LANES), jnp.float32),
            pltpu.VMEM((tile, RET_WIDTH), jnp.float32),
            pltpu.VMEM(w_in.shape, jnp.bfloat16),
            pltpu.VMEM(w_pw2.shape, jnp.bfloat16),
            pltpu.VMEM(w_out.shape, jnp.bfloat16),
            pltpu.VMEM((STAGE_SLOTS, WIDE_STAGE_ROWS, w_in.shape[1]), jnp.float32),
            pltpu.VMEM((STAGE_SLOTS, NARROW_STAGE_ROWS, w_out.shape[1]), jnp.float32),
            pltpu.SemaphoreType.DMA((STAGE_SLOTS,)),
            pltpu.SemaphoreType.DMA((STAGE_SLOTS,)),
        ],
        compiler_params=pltpu.CompilerParams(
            dimension_semantics=("arbitrary", "arbitrary"),
            vmem_limit_bytes=VMEM_LIMIT_BYTES),
        name="hybrid_layer",
    )(*args)


@jax.jit
def kernel(x, w_in, ret_norm_g, dw_kernel, dw_bias, conv_ln_g, conv_ln_b, w_pw2, b_pw2, w_out, post_ln_g, post_ln_b):
    for layer in range(DEPTH):
        x = _layer(x, w_in[layer], ret_norm_g[layer], dw_kernel[layer], dw_bias[layer],
                   conv_ln_g[layer], conv_ln_b[layer], w_pw2[layer], b_pw2[layer],
                   w_out[layer], post_ln_g[layer], post_ln_b[layer])
    return x
```
